```python
import math
import jax, jax.numpy as jnp
from jax import lax
import numpy as np

D_MODEL = 1024
BATCH = 4
SEQ = 8192
DEPTH = 2

GRID_W = 64
CTX_LEN = 256
EPS = 1e-6
ROPE_BASE = 10000.0
NEG_INF = -1e30
BLOCK_Q = 128

DA_HEADS = 4
DA_DIM = 64
RET_HEADS = 4
RET_QK = 64
RET_V = 128
RET_CHUNK = 128
DA_QK_W = DA_HEADS * 2 * DA_DIM
DA_V_W = DA_HEADS * 2 * DA_DIM
RET_QK_W = RET_HEADS * RET_QK
RET_V_W = RET_HEADS * RET_V
EVEN_WIDTHS = (DA_QK_W, DA_QK_W, DA_V_W, RET_QK_W, RET_QK_W, RET_V_W, RET_V_W)
EVEN_IN = DA_QK_W * 2 + DA_V_W + RET_QK_W * 2 + RET_V_W * 2
EVEN_OUT = DA_V_W + RET_V_W

SW_Q_HEADS = 16
SW_KV_HEADS = 2
SW_DIM = 64
WINDOW = 128
ODD_IN = (SW_Q_HEADS + 2 * SW_KV_HEADS) * SW_DIM
ODD_OUT = SW_Q_HEADS * SW_DIM

N_EXPERTS = 32
TOP_K = 4
D_FF = 1024
SWIGLU_ALPHA = 1.702
SWIGLU_LIMIT = 7.0
MOE_BLOCK = 128

kernel_name = 'hybrid_diffattn_retention_swa_moe_dit'


def rmsnorm(x, w):
    xf = x.astype(jnp.float32)
    y = xf * lax.rsqrt(jnp.mean(xf * xf, axis=-1, keepdims=True) + EPS)
    return (y * w.astype(jnp.float32)).astype(x.dtype)


def adaln(cvec, w, b):
    m = (jax.nn.silu(cvec) @ w + b)[..., None, :]
    return jnp.split(m, 6, axis=-1)


def axial_rope(x, row, col):
    hd = x.shape[-1]
    p = hd // 2
    f = p // 2
    inv = ROPE_BASE ** (-jnp.arange(f, dtype=jnp.float32) / f)
    xf = x.astype(jnp.float32)

    def rot(part, pos):
        ang = pos[:, None] * inv[None, :]
        cos = jnp.cos(ang)[:, None, :]
        sin = jnp.sin(ang)[:, None, :]
        a, b = part[..., :f], part[..., f:]
        return jnp.concatenate([a * cos - b * sin, a * sin + b * cos], axis=-1)

    out = jnp.concatenate([rot(xf[..., :p], row), rot(xf[..., p:], col)], axis=-1)
    return out.astype(x.dtype)


def split_heads(t, n_heads):
    B, N, _ = t.shape
    return t.reshape(B, N, n_heads, -1).transpose(0, 2, 1, 3)


def diff_attention(lat, ctx, lam, need_ctx):
    q, k, v = lat
    qc, kc, vc = ctx
    B, H, _, S, d = q.shape
    scale = d ** -0.5
    k_all = jnp.concatenate([kc, k], axis=3)
    v_all = jnp.concatenate([vc, v], axis=2)

    def attend(qb, keys, vals):
        s = jnp.einsum('bhcqd,bhckd->bhcqk', qb, keys).astype(jnp.float32) * scale
        p = jax.nn.softmax(s, axis=-1)
        a = p[:, :, 0] - lam * p[:, :, 1]
        return jnp.einsum('bhqk,bhke->bhqe', a.astype(vals.dtype), vals)

    nb = S // BLOCK_Q
    qb = jnp.moveaxis(q.reshape(B, H, 2, nb, BLOCK_Q, d), 3, 0)
    o = lax.map(lambda blk: attend(blk, k_all, v_all), qb)
    o_lat = jnp.moveaxis(o, 0, 2).reshape(B, H, S, 2 * d)
    o_ctx = attend(qc, kc, vc) if need_ctx else None
    return o_lat, o_ctx


def retention_chunkwise(q, k, v, log_gamma, state0):
    B, H, N, dk = q.shape
    dv = v.shape[-1]
    C = RET_CHUNK
    nc = N // C
    qc = q.reshape(B, H, nc, C, dk).astype(jnp.float32)
    kc = k.reshape(B, H, nc, C, dk).astype(jnp.float32)
    vc = v.reshape(B, H, nc, C, dv).astype(jnp.float32)
    i = jnp.arange(C, dtype=jnp.float32)
    lg = log_gamma[:, None, None]
    diff = i[:, None] - i[None, :]
    decay_in = jnp.where(diff >= 0, jnp.exp(lg * jnp.maximum(diff, 0.0)), 0.0)
    scores = jnp.einsum('bhnid,bhnjd->bhnij', qc, kc) * decay_in[:, None]
    inner = jnp.einsum('bhnij,bhnje->bhnie', scores, vc)
    k_dec = kc * jnp.exp(lg * (C - 1 - i))[..., None]
    chunk_kv = jnp.einsum('bhnjd,bhnje->nbhde', k_dec, vc)
    gamma_c = jnp.exp(log_gamma * C)[None, :, None, None]

    def step(state, kv):
        return gamma_c * state + kv, state

    final, prev_states = lax.scan(step, state0, chunk_kv)
    q_dec = qc * jnp.exp(lg * (i + 1))[..., None]
    cross = jnp.einsum('bhnid,nbhde->bhnie', q_dec, prev_states)
    return (inner + cross).reshape(B, H, N, dv), final


def bidir_retention(lat, ctx, decay_logit, need_ctx):
    q, k, v = lat
    qc, kc, vc = ctx
    B, H, _, dk = q.shape
    dv = v.shape[-1]
    log_gamma = jax.nn.log_sigmoid(decay_logit.astype(jnp.float32))
    zero = jnp.zeros((B, H, dk, dv), jnp.float32)
    flip = lambda t: t[:, :, ::-1]
    o_cf, s_f = retention_chunkwise(qc, kc, vc, log_gamma[0], zero)
    o_cb, s_b = retention_chunkwise(flip(qc), flip(kc), flip(vc), log_gamma[1], zero)
    o_lf, _ = retention_chunkwise(q, k, v, log_gamma[0], s_f)
    o_lb, _ = retention_chunkwise(flip(q), flip(k), flip(v), log_gamma[1], s_b)
    o_lat = (o_lf + flip(o_lb)).astype(v.dtype)
    o_ctx = (o_cf + flip(o_cb)).astype(v.dtype) if need_ctx else None
    return o_lat, o_ctx


def even_mixer(a_lat, a_ctx, w_in, w_out, lam_p, dnorm_w, decay_logit, rnorm_w, lam_init, row, col, need_ctx):
    cuts = np.cumsum(EVEN_WIDTHS)[:-1].tolist()

    def project(a, rope):
        B, N, _ = a.shape
        dq, dk, dv, rq, rk, rv, rg = jnp.split(a @ w_in, cuts, axis=-1)
        dq = dq.reshape(B, N, 2 * DA_HEADS, DA_DIM)
        dk = dk.reshape(B, N, 2 * DA_HEADS, DA_DIM)
        if rope:
            dq = axial_rope(dq, row, col)
            dk = axial_rope(dk, row, col)
        dq = dq.reshape(B, N, DA_HEADS, 2, DA_DIM).transpose(0, 2, 3, 1, 4)
        dk = dk.reshape(B, N, DA_HEADS, 2, DA_DIM).transpose(0, 2, 3, 1, 4)
        dv = split_heads(dv, DA_HEADS)
        rq = split_heads(rq, RET_HEADS) * (RET_QK ** -0.5)
        rk = split_heads(rk, RET_HEADS)
        rv = split_heads(rv, RET_HEADS)
        return (dq, dk, dv), (rq, rk, rv), rg

    da_lat, rt_lat, g_lat = project(a_lat, True)
    da_ctx, rt_ctx, g_ctx = project(a_ctx, False)
    lam = (jnp.exp(jnp.sum(lam_p[0] * lam_p[1])) - jnp.exp(jnp.sum(lam_p[2] * lam_p[3]))).astype(jnp.float32) + lam_init
    d_lat, d_ctx = diff_attention(da_lat, da_ctx, lam, need_ctx)
    r_lat, r_ctx = bidir_retention(rt_lat, rt_ctx, decay_logit, need_ctx)

    def finish(d, r, g):
        B, H, N, _ = d.shape
        d = rmsnorm(d, dnorm_w) * (1.0 - lam_init)
        r = rmsnorm(r, rnorm_w)
        d = d.transpose(0, 2, 1, 3).reshape(B, N, -1)
        r = r.transpose(0, 2, 1, 3).reshape(B, N, -1) * jax.nn.silu(g)
        return jnp.concatenate([d, r], axis=-1) @ w_out

    out_ctx = finish(d_ctx, r_ctx, g_ctx) if need_ctx else None
    return finish(d_lat, r_lat, g_lat), out_ctx


def window_attention(lat, ctx, sinks, need_ctx):
    q, k, v = lat
    qc, kc, vc = ctx
    B, Hkv, G, S, dh = q.shape
    scale = dh ** -0.5
    span = BLOCK_Q + 2 * WINDOW
    nb = S // BLOCK_Q
    pad = ((0, 0), (0, 0), (WINDOW, WINDOW), (0, 0))
    k_pad = jnp.pad(k, pad)
    v_pad = jnp.pad(v, pad)
    sink = sinks.astype(jnp.float32)[None, :, :, None, None]

    def softmax_with_sink(logits):
        full = jnp.concatenate([logits, jnp.broadcast_to(sink, logits.shape[:-1] + (1,))], axis=-1)
        return jax.nn.softmax(full, axis=-1)[..., :-1]

    def block(args):
        qb, n = args
        start = n * BLOCK_Q
        kb = lax.dynamic_slice_in_dim(k_pad, start, span, axis=2)
        vb = lax.dynamic_slice_in_dim(v_pad, start, span, axis=2)
        qpos = start + jnp.arange(BLOCK_Q)
        kpos = start - WINDOW + jnp.arange(span)
        valid = (jnp.abs(qpos[:, None] - kpos[None, :]) <= WINDOW) & (kpos >= 0) & (kpos < S)
        s_band = jnp.einsum('bhgqd,bhkd->bhgqk', qb, kb).astype(jnp.float32) * scale
        s_band = jnp.where(valid, s_band, NEG_INF)
        s_ctx = jnp.einsum('bhgqd,bhkd->bhgqk', qb, kc).astype(jnp.float32) * scale
        p = softmax_with_sink(jnp.concatenate([s_band, s_ctx], axis=-1)).astype(v.dtype)
        return (jnp.einsum('bhgqk,bhkd->bhgqd', p[..., :span], vb)
                + jnp.einsum('bhgqk,bhkd->bhgqd', p[..., span:], vc))

    qb = jnp.moveaxis(q.reshape(B, Hkv, G, nb, BLOCK_Q, dh), 3, 0)
    o = lax.map(block, (qb, jnp.arange(nb)))
    o_lat = jnp.moveaxis(o, 0, 3).reshape(B, Hkv, G, S, dh)
    o_ctx = None
    if need_ctx:
        s = jnp.einsum('bhgqd,bhkd->bhgqk', qc, kc).astype(jnp.float32) * scale
        o_ctx = jnp.einsum('bhgqk,bhkd->bhgqd', softmax_with_sink(s).astype(vc.dtype), vc)
    return o_lat, o_ctx


def odd_mixer(a_lat, a_ctx, w_qkv, w_out, sinks, row, col, need_ctx):
    G = SW_Q_HEADS // SW_KV_HEADS
    cuts = [SW_Q_HEADS * SW_DIM, (SW_Q_HEADS + SW_KV_HEADS) * SW_DIM]

    def project(a, rope):
        B, N, _ = a.shape
        q, k, v = jnp.split(a @ w_qkv, cuts, axis=-1)
        q = q.reshape(B, N, SW_Q_HEADS, SW_DIM)
        k = k.reshape(B, N, SW_KV_HEADS, SW_DIM)
        if rope:
            q = axial_rope(q, row, col)
            k = axial_rope(k, row, col)
        q = q.reshape(B, N, SW_KV_HEADS, G, SW_DIM).transpose(0, 2, 3, 1, 4)
        k = k.transpose(0, 2, 1, 3)
        v = split_heads(v, SW_KV_HEADS)
        return q, k, v

    o_lat, o_ctx = window_attention(project(a_lat, True), project(a_ctx, False),
                                    sinks.reshape(SW_KV_HEADS, G), need_ctx)

    def finish(o):
        B, _, _, N, _ = o.shape
        return o.transpose(0, 3, 1, 2, 4).reshape(B, N, -1) @ w_out

    return finish(o_lat), (finish(o_ctx) if need_ctx else None)


def moe(x, router_w, router_b, w1, b1, w2, b2):
    T, D = x.shape
    logits = (x @ router_w).astype(jnp.float32) + router_b.astype(jnp.float32)
    top_v, top_i = lax.top_k(logits, TOP_K)
    gate = jax.nn.softmax(top_v, axis=-1)
    M = T * TOP_K
    e_flat = top_i.reshape(M)
    tok_flat = jnp.repeat(jnp.arange(T, dtype=jnp.int32), TOP_K)
    order = jnp.argsort(e_flat)
    e_sorted = e_flat[order]
    counts = jnp.bincount(e_flat, length=N_EXPERTS)
    starts = jnp.cumsum(counts) - counts
    padded = ((counts + MOE_BLOCK - 1) // MOE_BLOCK) * MOE_BLOCK
    pad_ends = jnp.cumsum(padded)
    pad_starts = pad_ends - padded
    dest = pad_starts[e_sorted] + (jnp.arange(M) - starts[e_sorted])
    n_blocks = -(-M // MOE_BLOCK) + N_EXPERTS
    P = n_blocks * MOE_BLOCK
    src = jnp.full((P,), T, jnp.int32).at[dest].set(tok_flat[order])
    row_gate = jnp.zeros((P,), jnp.float32).at[dest].set(gate.reshape(M)[order])
    x_buf = jnp.concatenate([x, jnp.zeros((1, D), x.dtype)], axis=0)[src]
    block_expert = jnp.minimum(jnp.searchsorted(pad_ends, jnp.arange(n_blocks) * MOE_BLOCK, side='right'),
                               N_EXPERTS - 1)

    def expert_block(args):
        xb, e = args
        h = xb @ w1[e] + b1[e]
        glu = jnp.minimum(h[:, :D_FF], SWIGLU_LIMIT)
        lin = jnp.clip(h[:, D_FF:], -SWIGLU_LIMIT, SWIGLU_LIMIT)
        act = glu * jax.nn.sigmoid(SWIGLU_ALPHA * glu) * (lin + 1.0)
        return act @ w2[e] + b2[e]

    y_buf = lax.map(expert_block, (x_buf.reshape(n_blocks, MOE_BLOCK, D), block_expert))
    y = y_buf.reshape(P, D) * row_gate[:, None].astype(y_buf.dtype)
    return jax.ops.segment_sum(y, src, num_segments=T + 1)[:T]


def setup_inputs(seed: int = 0) -> dict:
    key = jax.random.key(seed)
    ks = list(jax.random.split(key, 32))
    n_even = (DEPTH + 1) // 2
    n_odd = DEPTH // 2

    def nrm(idx, shape, scale):
        return jax.random.normal(ks[idx], shape, jnp.float32) * scale

    g0 = 1.0 - 2.0 ** (-5.0 - np.arange(RET_HEADS))
    base_logit = jnp.asarray(np.log(g0 / (1.0 - g0)), jnp.float32)
    return {
        'x': nrm(0, (BATCH, SEQ, D_MODEL), 1.0),
        'c': nrm(1, (BATCH, D_MODEL), 1.0),
        'ctx': nrm(2, (BATCH, CTX_LEN, D_MODEL), 1.0),
        'c_ctx': nrm(3, (D_MODEL,), 1.0),
        'mod_w': nrm(4, (DEPTH, D_MODEL, 6 * D_MODEL), 0.5 * D_MODEL ** -0.5),
        'mod_b': nrm(5, (DEPTH, 6 * D_MODEL), 0.02),
        'norm1_w': 1.0 + nrm(6, (DEPTH, D_MODEL), 0.02),
        'norm2_w': 1.0 + nrm(7, (DEPTH, D_MODEL), 0.02),
        'even_w_in': nrm(8, (n_even, D_MODEL, EVEN_IN), D_MODEL ** -0.5),
        'even_w_out': nrm(9, (n_even, EVEN_OUT, D_MODEL), EVEN_OUT ** -0.5),
        'diff_lam': nrm(10, (n_even, 4, DA_DIM), 0.1),
        'diff_norm_w': 1.0 + nrm(11, (n_even, 2 * DA_DIM), 0.02),
        'ret_decay_logit': base_logit[None, None, :] + nrm(12, (n_even, 2, RET_HEADS), 0.05),
        'ret_norm_w': 1.0 + nrm(13, (n_even, RET_V), 0.02),
        'odd_w_qkv': nrm(14, (n_odd, D_MODEL, ODD_IN), D_MODEL ** -0.5),
        'odd_w_out': nrm(15, (n_odd, ODD_OUT, D_MODEL), ODD_OUT ** -0.5),
        'odd_sinks': nrm(16, (n_odd, SW_Q_HEADS), 0.5),
        'router_w': nrm(17, (DEPTH, D_MODEL, N_EXPERTS), D_MODEL ** -0.5),
        'router_b': nrm(18, (DEPTH, N_EXPERTS), 0.01),
        'moe_w1': nrm(19, (DEPTH, N_EXPERTS, D_MODEL, 2 * D_FF), D_MODEL ** -0.5),
        'moe_b1': nrm(20, (DEPTH, N_EXPERTS, 2 * D_FF), 0.01),
        'moe_w2': nrm(21, (DEPTH, N_EXPERTS, D_FF, D_MODEL), D_FF ** -0.5),
        'moe_b2': nrm(22, (DEPTH, N_EXPERTS, D_MODEL), 0.01),
        'final_norm_w': 1.0 + nrm(23, (D_MODEL,), 0.02),
    }


def reference(x, c, ctx, c_ctx, mod_w, mod_b, norm1_w, norm2_w,
              even_w_in, even_w_out, diff_lam, diff_norm_w, ret_decay_logit, ret_norm_w,
              odd_w_qkv, odd_w_out, odd_sinks,
              router_w, router_b, moe_w1, moe_b1, moe_w2, moe_b2, final_norm_w):
    B, S, D = x.shape
    L = ctx.shape[1]
    n_rows = S // GRID_W
    t = jnp.arange(n_rows * GRID_W)
    row = (t // GRID_W).astype(jnp.float32)
    col = (t % GRID_W).astype(jnp.float32)

    h_lat, h_ctx = x, ctx
    for layer in range(DEPTH):
        last = layer == DEPTH - 1
        sh1, sc1, g1, sh2, sc2, g2 = adaln(c, mod_w[layer], mod_b[layer])
        csh1, csc1, cg1, csh2, csc2, cg2 = adaln(c_ctx, mod_w[layer], mod_b[layer])
        a_lat = rmsnorm(h_lat, norm1_w[layer]) * (1.0 + sc1) + sh1
        a_ctx = rmsnorm(h_ctx, norm1_w[layer]) * (1.0 + csc1) + csh1
        if layer % 2 == 0:
            i = layer // 2
            lam_init = 0.8 - 0.6 * math.exp(-0.3 * layer)
            m_lat, m_ctx = even_mixer(a_lat, a_ctx, even_w_in[i], even_w_out[i], diff_lam[i], diff_norm_w[i],
                                      ret_decay_logit[i], ret_norm_w[i], lam_init, row, col, not last)
        else:
            i = layer // 2
            m_lat, m_ctx = odd_mixer(a_lat, a_ctx, odd_w_qkv[i], odd_w_out[i], odd_sinks[i], row, col, not last)
        h_lat = h_lat + g1 * m_lat
        f_lat = rmsnorm(h_lat, norm2_w[layer]) * (1.0 + sc2) + sh2
        if last:
            y = moe(f_lat.reshape(B * S, D), router_w[layer], router_b[layer],
                    moe_w1[layer], moe_b1[layer], moe_w2[layer], moe_b2[layer])
            h_lat = h_lat + g2 * y.reshape(B, S, D)
        else:
            h_ctx = h_ctx + cg1 * m_ctx
            f_ctx = rmsnorm(h_ctx, norm2_w[layer]) * (1.0 + csc2) + csh2
            tokens = jnp.concatenate([f_lat.reshape(B * S, D), f_ctx.reshape(B * L, D)], axis=0)
            y = moe(tokens, router_w[layer], router_b[layer],
                    moe_w1[layer], moe_b1[layer], moe_w2[layer], moe_b2[layer])
            h_lat = h_lat + g2 * y[:B * S].reshape(B, S, D)
            h_ctx = h_ctx + cg2 * y[B * S:].reshape(B, L, D)
    return rmsnorm(h_lat, final_norm_w)
```

```python
import functools
import math

import numpy as np
import jax
import jax.numpy as jnp
from jax import lax
from jax.experimental import pallas as pl
from jax.experimental.pallas import tpu as pltpu

F32 = jnp.float32
BF16 = jnp.bfloat16

GRID_W = 64
EPS = 1e-6
ROPE_BASE = 10000.0
NEG_INF = -1e30

DA_HEADS = 4
DA_DIM = 64
RET_HEADS = 4
RET_QK = 64
RET_V = 128
SW_Q_HEADS = 16
SW_KV_HEADS = 2
SW_DIM = 64
WINDOW = 128
N_EXPERTS = 32
TOP_K = 4
SWIGLU_ALPHA = 1.702
SWIGLU_LIMIT = 7.0

LANES = 128
ROW_TILE = 256
MOE_TILE = 256
FF_CHUNK = 512
VMEM_LIMIT = 48 * 1024 * 1024

E_DQ, E_DK, E_DV, E_RQ, E_RK, E_RV, E_RG = 0, 4, 8, 12, 14, 16, 20


def _cparams(sem):
    return pltpu.CompilerParams(dimension_semantics=sem, vmem_limit_bytes=VMEM_LIMIT)


def _dot(a, b):
    return jnp.dot(a, b, preferred_element_type=F32)


def _dot_nt(a, b):
    return lax.dot_general(a, b, (((1,), (1,)), ((), ())), preferred_element_type=F32)


def _dot_tn(a, b):
    return lax.dot_general(a, b, (((0,), (0,)), ((), ())), preferred_element_type=F32)


def _sigmoid(x):
    return 1.0 / (1.0 + jnp.exp(-x))


def _adaln_kernel(c_ref, w_ref, b_ref, o_ref):
    c = c_ref[...]
    s = c * _sigmoid(c)
    o_ref[...] = _dot(s.astype(BF16), w_ref[...].astype(BF16)) + b_ref[...]


def adaln_all(cvec, mod_w, mod_b):
    depth, d, d6 = mod_w.shape
    tn = 1024
    return pl.pallas_call(
        _adaln_kernel,
        grid=(depth, d6 // tn),
        in_specs=[
            pl.BlockSpec((8, d), lambda l, j: (0, 0)),
            pl.BlockSpec((None, d, tn), lambda l, j: (l, 0, j)),
            pl.BlockSpec((None, 1, tn), lambda l, j: (l, 0, j)),
        ],
        out_specs=pl.BlockSpec((None, 8, tn), lambda l, j: (l, 0, j)),
        out_shape=jax.ShapeDtypeStruct((depth, 8, d6), F32),
        compiler_params=_cparams(("arbitrary", "arbitrary")),
        name="adaln",
    )(cvec, mod_w, mod_b.reshape(depth, 1, d6))


def _norm_mod(x, nw, shift, scale):
    y = x * lax.rsqrt(jnp.mean(x * x, axis=-1, keepdims=True) + EPS)
    return (y * nw) * (1.0 + scale) + shift


def _proj_kernel(h_ref, mod_ref, nw_ref, w_ref, cos_ref, sa_ref, sb_ref, o_ref, *, groups, chunk):
    m = mod_ref[...]
    a = _norm_mod(h_ref[...], nw_ref[...], m[0:1], m[1:2]).astype(BF16)
    cos = cos_ref[...]
    sa = sa_ref[...]
    sb = sb_ref[...]
    gpc = chunk // LANES
    for c in range(len(groups) // gpc):
        acc = _dot(a, w_ref[:, c * chunk:(c + 1) * chunk])
        for gi in range(gpc):
            rope, scale = groups[c * gpc + gi]
            blk = acc[:, gi * LANES:(gi + 1) * LANES]
            if rope:
                blk = blk * cos + pltpu.roll(blk, LANES - 16, 1) * sa + pltpu.roll(blk, 16, 1) * sb
            if scale != 1.0:
                blk = blk * scale
            lo = (c * gpc + gi) * LANES
            o_ref[:, lo:lo + LANES] = blk.astype(BF16)


def proj_in(h, mod, nw, w, rope_tabs, groups, n_ctx_tiles):
    b, n, d = h.shape
    dout = w.shape[1]
    tm = ROW_TILE
    chunk = 512 if dout % 512 == 0 else 256
    cos, sa, sb = rope_tabs
    tab_spec = pl.BlockSpec((tm, LANES), lambda bi, i: (i, 0))
    return pl.pallas_call(
        functools.partial(_proj_kernel, groups=groups, chunk=chunk),
        grid=(b, n // tm),
        in_specs=[
            pl.BlockSpec((None, tm, d), lambda bi, i: (bi, i, 0)),
            pl.BlockSpec((None, None, 2, d), lambda bi, i: (bi, jnp.where(i < n_ctx_tiles, 0, 1), 0, 0)),
            pl.BlockSpec((1, d), lambda bi, i: (0, 0)),
            pl.BlockSpec((d, dout), lambda bi, i: (0, 0)),
            tab_spec, tab_spec, tab_spec,
        ],
        out_specs=pl.BlockSpec((None, tm, dout), lambda bi, i: (bi, i, 0)),
        out_shape=jax.ShapeDtypeStruct((b, n, dout), BF16),
        compiler_params=_cparams(("arbitrary", "arbitrary")),
        name="proj_in",
    )(h, mod, nw.reshape(1, d), w, cos, sa, sb)


def rope_tables(n_ctx, n_lat):
    f = DA_DIM // 4
    inv = ROPE_BASE ** (-jnp.arange(f, dtype=F32) / f)
    t = jnp.arange(n_lat)
    row = (t // GRID_W).astype(F32)
    col = (t % GRID_W).astype(F32)
    ang_r = row[:, None] * inv[None, :]
    ang_c = col[:, None] * inv[None, :]
    ang = jnp.concatenate([ang_r, ang_r, ang_c, ang_c], axis=-1)
    ang = jnp.concatenate([jnp.zeros((n_ctx, 4 * f), F32), ang], axis=0)
    cos = jnp.cos(ang)
    sin = jnp.sin(ang)
    first = (jnp.arange(4 * f) % (2 * f)) < f
    sa = jnp.where(first[None, :], -sin, 0.0)
    sb = jnp.where(first[None, :], 0.0, sin)
    rep = LANES // (4 * f)
    return tuple(jnp.tile(x, (1, rep)) for x in (cos, sa, sb))


def _dattn_kernel(q_ref, k_ref, v_ref, lam_ref, nw_ref, o_ref, m_sc, l_sc, acc_sc,
                  *, tk, n_ctx_tiles, n_ctx_chunks, lam_init):
    i = pl.program_id(2)
    n_all = k_ref.shape[0] // tk
    n_chunks = jnp.where(i < n_ctx_tiles, n_ctx_chunks, n_all)
    q = q_ref[...]
    lane = lax.broadcasted_iota(jnp.int32, q.shape, 1)
    zero = jnp.zeros_like(q)
    qs = (jnp.where(lane < DA_DIM, q, zero), jnp.where(lane >= DA_DIM, q, zero))
    m_sc[...] = jnp.full(m_sc.shape, -jnp.inf, F32)
    l_sc[...] = jnp.zeros(l_sc.shape, F32)
    acc_sc[...] = jnp.zeros(acc_sc.shape, F32)

    def body(c, carry):
        off = pl.multiple_of(c * tk, tk)
        k = k_ref[pl.ds(off, tk), :]
        v = v_ref[pl.ds(off, tk), :]
        for sub in range(2):
            s = _dot_nt(qs[sub], k)
            m_prev = m_sc[sub][:, :1]
            m_new = jnp.maximum(m_prev, jnp.max(s, axis=-1, keepdims=True))
            alpha = jnp.exp(m_prev - m_new)
            p = jnp.exp(s - m_new)
            l_sc[sub] = jnp.broadcast_to(alpha * l_sc[sub][:, :1] + jnp.sum(p, axis=-1, keepdims=True),
                                         l_sc.shape[1:])
            acc_sc[sub] = alpha * acc_sc[sub] + _dot(p.astype(BF16), v)
            m_sc[sub] = jnp.broadcast_to(m_new, m_sc.shape[1:])
        return carry

    lax.fori_loop(0, n_chunks, body, 0)

    lp = lam_ref[...]
    lam = (jnp.exp(jnp.sum(lp[0:1] * lp[1:2], axis=-1, keepdims=True))
           - jnp.exp(jnp.sum(lp[2:3] * lp[3:4], axis=-1, keepdims=True))) + lam_init
    o = acc_sc[0] / l_sc[0][:, :1] - lam * (acc_sc[1] / l_sc[1][:, :1])
    y = o * lax.rsqrt(jnp.mean(o * o, axis=-1, keepdims=True) + EPS)
    o_ref[...] = ((y * nw_ref[...]) * (1.0 - lam_init)).astype(o_ref.dtype)


def diff_attention(p0, lam_p, dnorm_w, lam_init, n_ctx):
    b, n, _ = p0.shape
    tq = ROW_TILE
    tk = 256
    hd = 2 * DA_DIM
    kern = functools.partial(_dattn_kernel, tk=tk, n_ctx_tiles=n_ctx // tq, n_ctx_chunks=n_ctx // tk,
                             lam_init=lam_init)
    return pl.pallas_call(
        kern,
        grid=(b, DA_HEADS, n // tq),
        in_specs=[
            pl.BlockSpec((None, tq, hd), lambda bi, h, i: (bi, i, E_DQ + h)),
            pl.BlockSpec((None, n, hd), lambda bi, h, i: (bi, 0, E_DK + h)),
            pl.BlockSpec((None, n, hd), lambda bi, h, i: (bi, 0, E_DV + h)),
            pl.BlockSpec((4, DA_DIM), lambda bi, h, i: (0, 0)),
            pl.BlockSpec((1, hd), lambda bi, h, i: (0, 0)),
        ],
        out_specs=pl.BlockSpec((None, tq, hd), lambda bi, h, i: (bi, i, h)),
        out_shape=jax.ShapeDtypeStruct((b, n, DA_HEADS * hd), BF16),
        scratch_shapes=[pltpu.VMEM((2, tq, LANES), F32), pltpu.VMEM((2, tq, LANES), F32),
                        pltpu.VMEM((2, tq, hd), F32)],
        compiler_params=_cparams(("arbitrary", "arbitrary", "arbitrary")),
        name="diff_attn",
    )(p0, p0, p0, lam_p, dnorm_w.reshape(1, hd))


def _ret_kernel(*refs, reverse, finalize):
    if finalize:
        q_ref, k_ref, v_ref, lg_ref, ob_ref, g_ref, nw_ref, o_ref, s_sc = refs
    else:
        q_ref, k_ref, v_ref, lg_ref, o_ref, s_sc = refs
    j = pl.program_id(1)
    c = q_ref.shape[0]

    @pl.when(j == 0)
    def _():
        s_sc[...] = jnp.zeros(s_sc.shape, F32)

    ri = lax.broadcasted_iota(jnp.int32, (c, c), 0)
    ci = lax.broadcasted_iota(jnp.int32, (c, c), 1)
    dist = (ci - ri) if reverse else (ri - ci)
    distf = jnp.maximum(dist, 0).astype(F32)
    pos = lax.broadcasted_iota(jnp.int32, (c, RET_QK), 0).astype(F32)
    q_exp = (c - pos) if reverse else (pos + 1.0)
    k_exp = pos if reverse else (c - 1.0 - pos)

    for h in range(RET_HEADS):
        x = lg_ref[h]
        lg = -jnp.log(1.0 + jnp.exp(-x))
        lgq = lg[:, :RET_QK]
        q = q_ref[:, h * RET_QK:(h + 1) * RET_QK].astype(F32)
        k = k_ref[:, h * RET_QK:(h + 1) * RET_QK].astype(F32)
        v = v_ref[:, h * RET_V:(h + 1) * RET_V]
        decay = jnp.where(dist >= 0, jnp.exp(lg[:, :1] * distf), 0.0)
        scores = _dot_nt(q.astype(BF16), k.astype(BF16)) * decay
        inner = _dot(scores.astype(BF16), v)
        q_dec = q * jnp.exp(lgq * q_exp)
        k_dec = k * jnp.exp(lgq * k_exp)
        state = s_sc[h]
        out = inner + _dot(q_dec.astype(BF16), state.astype(BF16))
        s_sc[h] = jnp.exp(lg * float(c)) * state + _dot_tn(k_dec.astype(BF16), v)
        if finalize:
            r = out + ob_ref[:, h * RET_V:(h + 1) * RET_V]
            y = r * lax.rsqrt(jnp.mean(r * r, axis=-1, keepdims=True) + EPS) * nw_ref[...]
            g = g_ref[:, h * RET_V:(h + 1) * RET_V].astype(F32)
            o_ref[:, h * RET_V:(h + 1) * RET_V] = (y * (g * _sigmoid(g))).astype(o_ref.dtype)
        else:
            o_ref[:, h * RET_V:(h + 1) * RET_V] = out


def retention(p0, decay_logit, rnorm_w):
    b, n, _ = p0.shape
    c = ROW_TILE
    nc = n // c
    logit = jnp.broadcast_to(decay_logit.astype(F32)[:, :, None, None], (2, RET_HEADS, 1, LANES))
    qw, vw = RET_HEADS * RET_QK, RET_HEADS * RET_V

    def specs(cmap):
        return [
            pl.BlockSpec((None, c, qw), lambda bi, j: (bi, cmap(j), E_RQ * LANES // qw)),
            pl.BlockSpec((None, c, qw), lambda bi, j: (bi, cmap(j), E_RK * LANES // qw)),
            pl.BlockSpec((None, c, vw), lambda bi, j: (bi, cmap(j), E_RV * LANES // vw)),
        ]

    lg_spec = lambda d: pl.BlockSpec((None, RET_HEADS, 1, LANES), lambda bi, j: (d, 0, 0, 0))
    scratch = [pltpu.VMEM((RET_HEADS, RET_QK, RET_V), F32)]
    bmap = lambda j: jnp.where(j == 0, 0, nc - j)
    o_b = pl.pallas_call(
        functools.partial(_ret_kernel, reverse=True, finalize=False),
        grid=(b, nc),
        in_specs=specs(bmap) + [lg_spec(1)],
        out_specs=pl.BlockSpec((None, c, vw), lambda bi, j: (bi, bmap(j), 0)),
        out_shape=jax.ShapeDtypeStruct((b, n, vw), F32),
        scratch_shapes=scratch,
        compiler_params=_cparams(("arbitrary", "arbitrary")),
        name="retention_bwd",
    )(p0, p0, p0, logit)
    fmap = lambda j: j
    return pl.pallas_call(
        functools.partial(_ret_kernel, reverse=False, finalize=True),
        grid=(b, nc),
        in_specs=specs(fmap) + [
            lg_spec(0),
            pl.BlockSpec((None, c, vw), lambda bi, j: (bi, j, 0)),
            pl.BlockSpec((None, c, vw), lambda bi, j: (bi, j, E_RG * LANES // vw)),
            pl.BlockSpec((1, RET_V), lambda bi, j: (0, 0)),
        ],
        out_specs=pl.BlockSpec((None, c, vw), lambda bi, j: (bi, j, 0)),
        out_shape=jax.ShapeDtypeStruct((b, n, vw), BF16),
        scratch_shapes=scratch,
        compiler_params=_cparams(("arbitrary", "arbitrary")),
        name="retention_fwd",
    )(p0, p0, p0, logit, o_b, p0, rnorm_w.reshape(1, RET_V))


def _wattn_kernel(q_ref, kc_ref, vc_ref, kp_ref, vp_ref, k0_ref, v0_ref, kn_ref, vn_ref, sink_ref, o_ref,
                  *, n_tiles, group):
    i = pl.program_id(1)
    tq = q_ref.shape[0]
    qi = lax.broadcasted_iota(jnp.int32, (tq, tq), 0)
    ki = lax.broadcasted_iota(jnp.int32, (tq, tq), 1)
    ok_prev = jnp.logical_and(qi + tq - ki <= WINDOW, i > 0)
    ok_cur = jnp.abs(qi - ki) <= WINDOW
    ok_next = jnp.logical_and(ki + tq - qi <= WINDOW, i < n_tiles - 1)
    for hk in range(SW_KV_HEADS):
        sl = slice(hk * SW_DIM, (hk + 1) * SW_DIM)
        kc, vc = kc_ref[:, sl], vc_ref[:, sl]
        kp, vp = kp_ref[:, sl], vp_ref[:, sl]
        k0, v0 = k0_ref[:, sl], v0_ref[:, sl]
        kn, vn = kn_ref[:, sl], vn_ref[:, sl]
        for g in range(group):
            col = (hk * group + g) * SW_DIM
            q = q_ref[:, col:col + SW_DIM]
            s_c = _dot_nt(q, kc)
            s_p = jnp.where(ok_prev, _dot_nt(q, kp), NEG_INF)
            s_0 = jnp.where(ok_cur, _dot_nt(q, k0), NEG_INF)
            s_n = jnp.where(ok_next, _dot_nt(q, kn), NEG_INF)
            sink = sink_ref[hk * group + g][:, :1]
            m = jnp.maximum(
                jnp.maximum(jnp.max(s_c, axis=-1, keepdims=True), jnp.max(s_p, axis=-1, keepdims=True)),
                jnp.maximum(jnp.max(s_0, axis=-1, keepdims=True), jnp.max(s_n, axis=-1, keepdims=True)))
            m = jnp.maximum(m, sink)
            p_c, p_p, p_0, p_n = (jnp.exp(s - m) for s in (s_c, s_p, s_0, s_n))
            den = (jnp.sum(p_c, axis=-1, keepdims=True) + jnp.sum(p_p, axis=-1, keepdims=True)
                   + jnp.sum(p_0, axis=-1, keepdims=True) + jnp.sum(p_n, axis=-1, keepdims=True)
                   + jnp.exp(sink - m))
            inv = 1.0 / den
            out = (_dot((p_c * inv).astype(BF16), vc) + _dot((p_p * inv).astype(BF16), vp)
                   + _dot((p_0 * inv).astype(BF16), v0) + _dot((p_n * inv).astype(BF16), vn))
            o_ref[:, col:col + SW_DIM] = out.astype(o_ref.dtype)


def window_attention(p1, sinks, n_ctx):
    b, n, _ = p1.shape
    tq = ROW_TILE
    assert n_ctx == tq and WINDOW <= tq
    n_tiles = (n - n_ctx) // tq
    group = SW_Q_HEADS // SW_KV_HEADS
    qw = SW_Q_HEADS * SW_DIM
    kvw = SW_KV_HEADS * SW_DIM
    assert kvw == LANES
    k_col = qw // kvw
    v_col = k_col + 1
    sink_b = jnp.broadcast_to(sinks.astype(F32).reshape(SW_Q_HEADS, 1, 1), (SW_Q_HEADS, 1, LANES))

    def kv(col, tile):
        return pl.BlockSpec((None, tq, kvw), lambda bi, i: (bi, tile(i), col))

    prev = lambda i: jnp.maximum(i, 1)
    cur = lambda i: i + 1
    nxt = lambda i: jnp.minimum(i + 2, n_tiles)
    ctx = lambda i: 0
    return pl.pallas_call(
        functools.partial(_wattn_kernel, n_tiles=n_tiles, group=group),
        grid=(b, n_tiles),
        in_specs=[
            pl.BlockSpec((None, tq, qw), lambda bi, i: (bi, i + 1, 0)),
            kv(k_col, ctx), kv(v_col, ctx), kv(k_col, prev), kv(v_col, prev),
            kv(k_col, cur), kv(v_col, cur), kv(k_col, nxt), kv(v_col, nxt),
            pl.BlockSpec((SW_Q_HEADS, 1, LANES), lambda bi, i: (0, 0, 0)),
        ],
        out_specs=pl.BlockSpec((None, tq, qw), lambda bi, i: (bi, i, 0)),
        out_shape=jax.ShapeDtypeStruct((b, n - n_ctx, qw), BF16),
        compiler_params=_cparams(("arbitrary", "arbitrary")),
        name="window_attn",
    )(p1, p1, p1, p1, p1, p1, p1, p1, p1, sink_b)


def _post_kernel(*refs, n_pieces):
    pieces = refs[:n_pieces]
    w_ref, h_ref, mod_ref, nw_ref, rw_ref, rb_ref, ho_ref, f_ref, lg_ref = refs[n_pieces:]
    m = None
    lo = 0
    for p in pieces:
        wdt = p.shape[1]
        t = _dot(p[...], w_ref[lo:lo + wdt, :])
        m = t if m is None else m + t
        lo += wdt
    md = mod_ref[...]
    hn = h_ref[...] + md[0:1] * m
    ho_ref[...] = hn
    f = _norm_mod(hn, nw_ref[...], md[1:2], md[2:3]).astype(BF16)
    f_ref[...] = f
    lg_ref[...] = _dot(f, rw_ref[...]) + rb_ref[...]


def post_mixer(pieces, w_out, h, mod, nw, router_w, router_b, n_ctx_tiles, row_off):
    b, r, _ = pieces[0].shape
    d = h.shape[2]
    tm = ROW_TILE
    rw = jnp.zeros((d, LANES), BF16).at[:, :N_EXPERTS].set(router_w.astype(BF16))
    rb = jnp.full((1, LANES), NEG_INF, F32).at[0, :N_EXPERTS].set(router_b.astype(F32))
    row = lambda bi, i: (bi, i, 0)
    const2 = lambda bi, i: (0, 0)
    outs = pl.pallas_call(
        functools.partial(_post_kernel, n_pieces=len(pieces)),
        grid=(b, r // tm),
        in_specs=[pl.BlockSpec((None, tm, p.shape[2]), row) for p in pieces] + [
            pl.BlockSpec(w_out.shape, const2),
            pl.BlockSpec((None, tm, d), lambda bi, i: (bi, i + row_off, 0)),
            pl.BlockSpec((None, None, 3, d),
                         lambda bi, i: (bi, jnp.where(i + row_off < n_ctx_tiles, 0, 1), 0, 0)),
            pl.BlockSpec((1, d), const2),
            pl.BlockSpec((d, LANES), const2),
            pl.BlockSpec((1, LANES), const2),
        ],
        out_specs=[pl.BlockSpec((None, tm, d), row), pl.BlockSpec((None, tm, d), row),
                   pl.BlockSpec((None, tm, LANES), row)],
        out_shape=[jax.ShapeDtypeStruct((b, r, d), F32), jax.ShapeDtypeStruct((b, r, d), BF16),
                   jax.ShapeDtypeStruct((b, r, LANES), F32)],
        compiler_params=_cparams(("arbitrary", "arbitrary")),
        name="post_mixer",
    )(*pieces, w_out, h, mod, nw.reshape(1, d), rw, rb)
    return outs


def _ffn_kernel(be_ref, nb_ref, x_ref, w1_ref, b1_ref, w2_ref, b2_ref, o_ref):
    i = pl.program_id(0)

    @pl.when(i < nb_ref[0])
    def _():
        x = x_ref[...]
        dff = w2_ref.shape[0]
        acc = None
        for c in range(dff // FF_CHUNK):
            lo = c * FF_CHUNK
            hg = _dot(x, w1_ref[:, lo:lo + FF_CHUNK]) + b1_ref[:, lo:lo + FF_CHUNK]
            hl = _dot(x, w1_ref[:, dff + lo:dff + lo + FF_CHUNK]) + b1_ref[:, dff + lo:dff + lo + FF_CHUNK]
            glu = jnp.minimum(hg, SWIGLU_LIMIT)
            lin = jnp.clip(hl, -SWIGLU_LIMIT, SWIGLU_LIMIT)
            act = glu * _sigmoid(SWIGLU_ALPHA * glu) * (lin + 1.0)
            t = _dot(act.astype(BF16), w2_ref[lo:lo + FF_CHUNK, :])
            acc = t if acc is None else acc + t
        o_ref[...] = (acc + b2_ref[...]).astype(o_ref.dtype)

    @pl.when(i >= nb_ref[0])
    def _():
        o_ref[...] = jnp.zeros(o_ref.shape, o_ref.dtype)


def expert_ffn(x_buf, block_expert, n_used, w1, b1, w2, b2):
    p, d = x_buf.shape
    ne, _, f2 = w1.shape
    dff = w2.shape[1]
    tm = MOE_TILE
    grid_spec = pltpu.PrefetchScalarGridSpec(
        num_scalar_prefetch=2,
        grid=(p // tm,),
        in_specs=[
            pl.BlockSpec((tm, d), lambda i, be, nb: (i, 0)),
            pl.BlockSpec((None, d, f2), lambda i, be, nb: (be[i], 0, 0)),
            pl.BlockSpec((None, 1, f2), lambda i, be, nb: (be[i], 0, 0)),
            pl.BlockSpec((None, dff, d), lambda i, be, nb: (be[i], 0, 0)),
            pl.BlockSpec((None, 1, d), lambda i, be, nb: (be[i], 0, 0)),
        ],
        out_specs=pl.BlockSpec((tm, d), lambda i, be, nb: (i, 0)),
    )
    return pl.pallas_call(
        _ffn_kernel,
        grid_spec=grid_spec,
        out_shape=jax.ShapeDtypeStruct((p, d), BF16),
        compiler_params=_cparams(("arbitrary",)),
        name="expert_ffn",
    )(block_expert, n_used, x_buf, w1, b1.reshape(ne, 1, f2), w2, b2.reshape(ne, 1, d))


def route(logits):
    t = logits.shape[0]
    top_v, top_i = lax.top_k(logits[:, :N_EXPERTS], TOP_K)
    gate = jax.nn.softmax(top_v, axis=-1)
    m = t * TOP_K
    e_flat = top_i.reshape(m)
    onehot = (e_flat[:, None] == jnp.arange(N_EXPERTS, dtype=e_flat.dtype)[None, :]).astype(jnp.int32)
    csum = jnp.cumsum(onehot, axis=0)
    rank = jnp.take_along_axis(csum, e_flat[:, None], axis=1)[:, 0] - 1
    counts = csum[-1]
    padded = ((counts + MOE_TILE - 1) // MOE_TILE) * MOE_TILE
    pad_ends = jnp.cumsum(padded)
    pad_starts = pad_ends - padded
    dest = pad_starts[e_flat] + rank
    n_blocks = -(-m // MOE_TILE) + N_EXPERTS
    cap = n_blocks * MOE_TILE
    tok = jnp.repeat(jnp.arange(t, dtype=jnp.int32), TOP_K)
    src = jnp.zeros((cap,), jnp.int32).at[dest].set(tok)
    block_expert = jnp.minimum(
        jnp.searchsorted(pad_ends, jnp.arange(n_blocks, dtype=jnp.int32) * MOE_TILE, side='right'),
        N_EXPERTS - 1).astype(jnp.int32)
    n_used = (pad_ends[-1] // MOE_TILE).astype(jnp.int32).reshape(1)
    return gate, dest.reshape(t, TOP_K).astype(jnp.int32), src, block_expert, n_used


def _combine_kernel(*refs, final):
    if final:
        h_ref, y_ref, gt_ref, g2_ref, nw_ref, o_ref = refs
    else:
        h_ref, y_ref, gt_ref, g2_ref, o_ref = refs
    gt = gt_ref[...]
    acc = None
    for k in range(TOP_K):
        t = gt[:, k:k + 1] * y_ref[k].astype(F32)
        acc = t if acc is None else acc + t
    hn = h_ref[...] + g2_ref[...] * acc
    if final:
        y = hn * lax.rsqrt(jnp.mean(hn * hn, axis=-1, keepdims=True) + EPS)
        hn = y * nw_ref[...]
    o_ref[...] = hn


def combine(h, y_g, gate, g2, n_ctx_tiles, final_w=None):
    b, r, d = h.shape
    tm = ROW_TILE
    final = final_w is not None
    in_specs = [
        pl.BlockSpec((None, tm, d), lambda bi, i: (bi, i, 0)),
        pl.BlockSpec((TOP_K, None, tm, d), lambda bi, i: (0, bi, i, 0)),
        pl.BlockSpec((None, tm, TOP_K), lambda bi, i: (bi, i, 0)),
        pl.BlockSpec((None, None, 1, d), lambda bi, i: (bi, jnp.where(i < n_ctx_tiles, 0, 1), 0, 0)),
    ]
    args = [h, y_g, gate, g2]
    if final:
        in_specs.append(pl.BlockSpec((1, d), lambda bi, i: (0, 0)))
        args.append(final_w.reshape(1, d))
    return pl.pallas_call(
        functools.partial(_combine_kernel, final=final),
        grid=(b, r // tm),
        in_specs=in_specs,
        out_specs=pl.BlockSpec((None, tm, d), lambda bi, i: (bi, i, 0)),
        out_shape=jax.ShapeDtypeStruct((b, r, d), F32),
        compiler_params=_cparams(("arbitrary", "arbitrary")),
        name="combine",
    )(*args)


def moe_layer(h, f, logits, g2, n_ctx_tiles, w1, b1, w2, b2, final_w=None):
    b, r, d = h.shape
    t = b * r
    gate, dest, src, block_expert, n_used = route(logits.reshape(t, LANES))
    x_buf = jnp.take(f.reshape(t, d), src, axis=0)
    y_buf = expert_ffn(x_buf, block_expert, n_used, w1, b1, w2, b2)
    y_g = jnp.take(y_buf, dest.T, axis=0).reshape(TOP_K, b, r, d)
    return combine(h, y_g, gate.reshape(b, r, TOP_K), g2, n_ctx_tiles, final_w)


def _mod_table(rows, n_batch):
    lat = jnp.stack([r[:n_batch] for r in rows], axis=1)
    ctx = jnp.stack([r[n_batch] for r in rows], axis=0)
    return jnp.stack([jnp.broadcast_to(ctx[None], lat.shape), lat], axis=1)


def kernel(x, c, ctx, c_ctx, mod_w, mod_b, norm1_w, norm2_w, even_w_in, even_w_out, diff_lam, diff_norm_w,
           ret_decay_logit, ret_norm_w, odd_w_qkv, odd_w_out, odd_sinks, router_w, router_b,
           moe_w1, moe_b1, moe_w2, moe_b2, final_norm_w):
    b, s, d = x.shape
    n_ctx = ctx.shape[1]
    n_ctx_tiles = n_ctx // ROW_TILE
    assert n_ctx % ROW_TILE == 0 and s % ROW_TILE == 0 and b + 1 <= 8

    cvec = jnp.zeros((8, d), F32).at[:b].set(c).at[b].set(c_ctx)
    mods = adaln_all(cvec, mod_w, mod_b)
    rope_tabs = rope_tables(n_ctx, s)
    w1 = moe_w1.astype(BF16)
    w2 = moe_w2.astype(BF16)

    sh1, sc1, g1, sh2, sc2, g2 = jnp.split(mods[0], 6, axis=-1)
    h = jnp.concatenate([ctx, x], axis=1)
    groups0 = ((True, 0.125),) * 4 + ((True, 1.0),) * 4 + ((False, 1.0),) * 4 \
        + ((False, 0.125),) * 2 + ((False, 1.0),) * 10
    p0 = proj_in(h, _mod_table([sh1, sc1], b), norm1_w[0], even_w_in[0].astype(BF16), rope_tabs, groups0,
                 n_ctx_tiles)
    lam_init = 0.8 - 0.6 * math.exp(-0.3 * 0)
    d_out = diff_attention(p0, diff_lam[0], diff_norm_w[0], lam_init, n_ctx)
    r_out = retention(p0, ret_decay_logit[0], ret_norm_w[0])
    h, f, logits = post_mixer([d_out, r_out], even_w_out[0].astype(BF16), h, _mod_table([g1, sh2, sc2], b),
                              norm2_w[0], router_w[0], router_b[0], n_ctx_tiles, 0)
    h = moe_layer(h, f, logits, _mod_table([g2], b), n_ctx_tiles, w1[0], moe_b1[0], w2[0], moe_b2[0])

    sh1, sc1, g1, sh2, sc2, g2 = jnp.split(mods[1], 6, axis=-1)
    groups1 = ((True, 0.125),) * 8 + ((True, 1.0),) + ((False, 1.0),)
    p1 = proj_in(h, _mod_table([sh1, sc1], b), norm1_w[1], odd_w_qkv[0].astype(BF16), rope_tabs, groups1,
                 n_ctx_tiles)
    a_out = window_attention(p1, odd_sinks[0], n_ctx)
    h_lat, f, logits = post_mixer([a_out], odd_w_out[0].astype(BF16), h, _mod_table([g1, sh2, sc2], b),
                                  norm2_w[1], router_w[1], router_b[1], n_ctx_tiles, n_ctx_tiles)
    return moe_layer(h_lat, f, logits, _mod_table([g2], b), 0, w1[1], moe_b1[1], w2[1], moe_b2[1],
                     final_w=final_norm_w)
```

```python
import functools
import math

import jax
import jax.numpy as jnp
from jax import lax
from jax.experimental import pallas as pl
from jax.experimental.pallas import tpu as pltpu

F32 = jnp.float32
BF16 = jnp.bfloat16

GRID_W = 64
EPS = 1e-6
ROPE_BASE = 10000.0
NEG_INF = -1e30

DA_HEADS = 4
DA_DIM = 64
RET_HEADS = 4
RET_QK = 64
RET_V = 128
SW_Q_HEADS = 16
SW_KV_HEADS = 2
SW_DIM = 64
WINDOW = 128
N_EXPERTS = 32
TOP_K = 4
SWIGLU_ALPHA = 1.702
SWIGLU_LIMIT = 7.0

LANES = 128
ROW_TILE = 256
MOE_TILE = 256
DATTN_TQ = 512
DATTN_TK = 768
FF_CHUNK = 512
VMEM_LIMIT = 48 * 1024 * 1024

E_DQ, E_DK, E_DV, E_RQ, E_RK, E_RV, E_RG = 0, 4, 8, 12, 14, 16, 20


def _cparams(sem):
    return pltpu.CompilerParams(dimension_semantics=sem, vmem_limit_bytes=VMEM_LIMIT)


def _dot(a, b):
    return jnp.dot(a, b, preferred_element_type=F32)


def _dot_nt(a, b):
    return lax.dot_general(a, b, (((1,), (1,)), ((), ())), preferred_element_type=F32)


def _dot_tn(a, b):
    return lax.dot_general(a, b, (((0,), (0,)), ((), ())), preferred_element_type=F32)


def _sigmoid(x):
    return 1.0 / (1.0 + jnp.exp(-x))


def _adaln_kernel(c_ref, w_ref, b_ref, o_ref):
    c = c_ref[...]
    s = c * _sigmoid(c)
    o_ref[...] = _dot(s.astype(BF16), w_ref[...].astype(BF16)) + b_ref[...]


def adaln_all(cvec, mod_w, mod_b):
    depth, d, d6 = mod_w.shape
    tn = 1024
    return pl.pallas_call(
        _adaln_kernel,
        grid=(depth, d6 // tn),
        in_specs=[
            pl.BlockSpec((8, d), lambda l, j: (0, 0)),
            pl.BlockSpec((None, d, tn), lambda l, j: (l, 0, j)),
            pl.BlockSpec((None, 1, tn), lambda l, j: (l, 0, j)),
        ],
        out_specs=pl.BlockSpec((None, 8, tn), lambda l, j: (l, 0, j)),
        out_shape=jax.ShapeDtypeStruct((depth, 8, d6), F32),
        compiler_params=_cparams(("arbitrary", "arbitrary")),
        name="adaln",
    )(cvec, mod_w, mod_b.reshape(depth, 1, d6))


def _norm_mod(x, nw, shift, scale):
    y = x * lax.rsqrt(jnp.mean(x * x, axis=-1, keepdims=True) + EPS)
    return (y * nw) * (1.0 + scale) + shift


def _proj_kernel(h_ref, mod_ref, nw_ref, w_ref, cos_ref, sa_ref, sb_ref, o_ref, *, groups, chunk):
    m = mod_ref[...]
    a = _norm_mod(h_ref[...], nw_ref[...], m[0:1], m[1:2]).astype(BF16)
    cos = cos_ref[...]
    sa = sa_ref[...]
    sb = sb_ref[...]
    gpc = chunk // LANES
    for c in range(len(groups) // gpc):
        acc = _dot(a, w_ref[:, c * chunk:(c + 1) * chunk])
        for gi in range(gpc):
            rope, scale = groups[c * gpc + gi]
            blk = acc[:, gi * LANES:(gi + 1) * LANES]
            if rope:
                blk = blk * cos + pltpu.roll(blk, LANES - 16, 1) * sa + pltpu.roll(blk, 16, 1) * sb
            if scale != 1.0:
                blk = blk * scale
            lo = (c * gpc + gi) * LANES
            o_ref[:, lo:lo + LANES] = blk.astype(BF16)


def proj_in(h, mod, nw, w, rope_tabs, groups, n_lat_tiles):
    b, n, d = h.shape
    dout = w.shape[1]
    tm = ROW_TILE
    chunk = 512 if dout % 512 == 0 else 256
    cos, sa, sb = rope_tabs
    tab_spec = pl.BlockSpec((tm, LANES), lambda bi, i: (i, 0))
    return pl.pallas_call(
        functools.partial(_proj_kernel, groups=groups, chunk=chunk),
        grid=(b, n // tm),
        in_specs=[
            pl.BlockSpec((None, tm, d), lambda bi, i: (bi, i, 0)),
            pl.BlockSpec((None, None, 2, d), lambda bi, i: (bi, jnp.where(i < n_lat_tiles, 1, 0), 0, 0)),
            pl.BlockSpec((1, d), lambda bi, i: (0, 0)),
            pl.BlockSpec((d, dout), lambda bi, i: (0, 0)),
            tab_spec, tab_spec, tab_spec,
        ],
        out_specs=pl.BlockSpec((None, tm, dout), lambda bi, i: (bi, i, 0)),
        out_shape=jax.ShapeDtypeStruct((b, n, dout), BF16),
        compiler_params=_cparams(("arbitrary", "arbitrary")),
        name="proj_in",
    )(h, mod, nw.reshape(1, d), w, cos, sa, sb)


def rope_tables(n_lat, n_ctx):
    f = DA_DIM // 4
    inv = ROPE_BASE ** (-jnp.arange(f, dtype=F32) / f)
    t = jnp.arange(n_lat)
    row = (t // GRID_W).astype(F32)
    col = (t % GRID_W).astype(F32)
    ang_r = row[:, None] * inv[None, :]
    ang_c = col[:, None] * inv[None, :]
    ang = jnp.concatenate([ang_r, ang_r, ang_c, ang_c], axis=-1)
    ang = jnp.concatenate([ang, jnp.zeros((n_ctx, 4 * f), F32)], axis=0)
    cos = jnp.cos(ang)
    sin = jnp.sin(ang)
    first = (jnp.arange(4 * f) % (2 * f)) < f
    sa = jnp.where(first[None, :], -sin, 0.0)
    sb = jnp.where(first[None, :], 0.0, sin)
    rep = LANES // (4 * f)
    return tuple(jnp.tile(x, (1, rep)) for x in (cos, sa, sb))


def _dattn_kernel(q_ref, k_ref, v_ref, lam_ref, nw_ref, o_ref, acc_sc, *, tk, lam_init):
    tq = q_ref.shape[0]
    n_chunks = k_ref.shape[0] // tk
    q = q_ref[...]
    lane = lax.broadcasted_iota(jnp.int32, q.shape, 1)
    zero = jnp.zeros_like(q)
    qs = (jnp.where(lane < DA_DIM, q, zero), jnp.where(lane >= DA_DIM, q, zero))
    acc_sc[...] = jnp.zeros(acc_sc.shape, F32)

    def body(c, carry):
        ms, ls = carry
        off = pl.multiple_of(c * tk, tk)
        k = k_ref[pl.ds(off, tk), :]
        v = v_ref[pl.ds(off, tk), :]
        new_m, new_l = [], []
        for sub in range(2):
            s = _dot_nt(k, qs[sub])
            m_new = jnp.maximum(ms[sub], jnp.max(s, axis=0, keepdims=True))
            alpha = jnp.exp(ms[sub] - m_new)
            p = jnp.exp(s - m_new)
            new_l.append(alpha * ls[sub] + jnp.sum(p, axis=0, keepdims=True))
            acc_sc[sub] = alpha * acc_sc[sub] + _dot_tn(v, p.astype(BF16))
            new_m.append(m_new)
        return tuple(new_m), tuple(new_l)

    m0 = jnp.full((1, tq), -jnp.inf, F32)
    l0 = jnp.zeros((1, tq), F32)
    _, ls = lax.fori_loop(0, n_chunks, body, ((m0, m0), (l0, l0)))

    lp = lam_ref[...]
    lam = (jnp.exp(jnp.sum(lp[0:1] * lp[1:2], axis=-1, keepdims=True))
           - jnp.exp(jnp.sum(lp[2:3] * lp[3:4], axis=-1, keepdims=True))) + lam_init
    o = acc_sc[0] * (1.0 / ls[0]) - lam * (acc_sc[1] * (1.0 / ls[1]))
    y = o * lax.rsqrt(jnp.mean(o * o, axis=0, keepdims=True) + EPS)
    y = (y * nw_ref[...]) * (1.0 - lam_init)
    o_ref[...] = y.T.astype(o_ref.dtype)


def _dattn_call(p0, lam_p, dnorm_w, lam_init, tq, tk, q_tiles, q_tile0, kv_rows, kv_tile0):
    b = p0.shape[0]
    hd = 2 * DA_DIM
    return pl.pallas_call(
        functools.partial(_dattn_kernel, tk=tk, lam_init=lam_init),
        grid=(b, DA_HEADS, q_tiles),
        in_specs=[
            pl.BlockSpec((None, tq, hd), lambda bi, h, i: (bi, i + q_tile0, E_DQ + h)),
            pl.BlockSpec((None, kv_rows, hd), lambda bi, h, i: (bi, kv_tile0, E_DK + h)),
            pl.BlockSpec((None, kv_rows, hd), lambda bi, h, i: (bi, kv_tile0, E_DV + h)),
            pl.BlockSpec((4, DA_DIM), lambda bi, h, i: (0, 0)),
            pl.BlockSpec((hd, 1), lambda bi, h, i: (0, 0)),
        ],
        out_specs=pl.BlockSpec((None, tq, hd), lambda bi, h, i: (bi, i, h)),
        out_shape=jax.ShapeDtypeStruct((b, q_tiles * tq, DA_HEADS * hd), BF16),
        scratch_shapes=[pltpu.VMEM((2, hd, tq), F32)],
        compiler_params=_cparams(("arbitrary", "arbitrary", "arbitrary")),
        name="diff_attn",
    )(p0, p0, p0, lam_p, dnorm_w.reshape(hd, 1))


def diff_attention(p0, lam_p, dnorm_w, lam_init, n_ctx):
    n = p0.shape[1]
    s = n - n_ctx
    tq = min(DATTN_TQ, s)
    tk = DATTN_TK if n % DATTN_TK == 0 else n_ctx
    assert s % tq == 0 and n % tk == 0 and s % n_ctx == 0
    lat = _dattn_call(p0, lam_p, dnorm_w, lam_init, tq, tk, s // tq, 0, n, 0)
    ctx = _dattn_call(p0, lam_p, dnorm_w, lam_init, n_ctx, n_ctx, 1, s // n_ctx, n_ctx, s // n_ctx)
    return jnp.concatenate([lat, ctx], axis=1)


def _ret_kernel(*refs, reverse, finalize):
    if finalize:
        q_ref, k_ref, v_ref, lg_ref, ob_ref, g_ref, nw_ref, o_ref, s_sc = refs
    else:
        q_ref, k_ref, v_ref, lg_ref, o_ref, s_sc = refs
    j = pl.program_id(1)
    c = q_ref.shape[0]

    @pl.when(j == 0)
    def _():
        s_sc[...] = jnp.zeros(s_sc.shape, F32)

    ri = lax.broadcasted_iota(jnp.int32, (c, c), 0)
    ci = lax.broadcasted_iota(jnp.int32, (c, c), 1)
    dist = (ci - ri) if reverse else (ri - ci)
    distf = jnp.maximum(dist, 0).astype(F32)
    pos = lax.broadcasted_iota(jnp.int32, (c, RET_QK), 0).astype(F32)
    q_exp = (c - pos) if reverse else (pos + 1.0)
    k_exp = pos if reverse else (c - 1.0 - pos)

    for h in range(RET_HEADS):
        x = lg_ref[h]
        lg = -jnp.log(1.0 + jnp.exp(-x))
        lgq = lg[:, :RET_QK]
        q = q_ref[:, h * RET_QK:(h + 1) * RET_QK].astype(F32)
        k = k_ref[:, h * RET_QK:(h + 1) * RET_QK].astype(F32)
        v = v_ref[:, h * RET_V:(h + 1) * RET_V]
        decay = jnp.where(dist >= 0, jnp.exp(lg[:, :1] * distf), 0.0)
        scores = _dot_nt(q.astype(BF16), k.astype(BF16)) * decay
        inner = _dot(scores.astype(BF16), v)
        q_dec = q * jnp.exp(lgq * q_exp)
        k_dec = k * jnp.exp(lgq * k_exp)
        state = s_sc[h]
        out = inner + _dot(q_dec.astype(BF16), state.astype(BF16))
        s_sc[h] = jnp.exp(lg * float(c)) * state + _dot_tn(k_dec.astype(BF16), v)
        if finalize:
            r = out + ob_ref[:, h * RET_V:(h + 1) * RET_V]
            y = r * lax.rsqrt(jnp.mean(r * r, axis=-1, keepdims=True) + EPS) * nw_ref[...]
            g = g_ref[:, h * RET_V:(h + 1) * RET_V].astype(F32)
            o_ref[:, h * RET_V:(h + 1) * RET_V] = (y * (g * _sigmoid(g))).astype(o_ref.dtype)
        else:
            o_ref[:, h * RET_V:(h + 1) * RET_V] = out


def retention(p0, decay_logit, rnorm_w):
    b, n, _ = p0.shape
    c = ROW_TILE
    nc = n // c
    logit = jnp.broadcast_to(decay_logit.astype(F32)[:, :, None, None], (2, RET_HEADS, 1, LANES))
    qw, vw = RET_HEADS * RET_QK, RET_HEADS * RET_V

    def specs(cmap):
        return [
            pl.BlockSpec((None, c, qw), lambda bi, j: (bi, cmap(j), E_RQ * LANES // qw)),
            pl.BlockSpec((None, c, qw), lambda bi, j: (bi, cmap(j), E_RK * LANES // qw)),
            pl.BlockSpec((None, c, vw), lambda bi, j: (bi, cmap(j), E_RV * LANES // vw)),
        ]

    lg_spec = lambda d: pl.BlockSpec((None, RET_HEADS, 1, LANES), lambda bi, j: (d, 0, 0, 0))
    scratch = [pltpu.VMEM((RET_HEADS, RET_QK, RET_V), F32)]
    bmap = lambda j: jnp.where(j == 0, nc - 1, nc - 1 - j)
    o_b = pl.pallas_call(
        functools.partial(_ret_kernel, reverse=True, finalize=False),
        grid=(b, nc),
        in_specs=specs(bmap) + [lg_spec(1)],
        out_specs=pl.BlockSpec((None, c, vw), lambda bi, j: (bi, bmap(j), 0)),
        out_shape=jax.ShapeDtypeStruct((b, n, vw), F32),
        scratch_shapes=scratch,
        compiler_params=_cparams(("arbitrary", "arbitrary")),
        name="retention_bwd",
    )(p0, p0, p0, logit)
    fmap = lambda j: jnp.where(j == 0, nc - 1, j - 1)
    return pl.pallas_call(
        functools.partial(_ret_kernel, reverse=False, finalize=True),
        grid=(b, nc),
        in_specs=specs(fmap) + [
            lg_spec(0),
            pl.BlockSpec((None, c, vw), lambda bi, j: (bi, fmap(j), 0)),
            pl.BlockSpec((None, c, vw), lambda bi, j: (bi, fmap(j), E_RG * LANES // vw)),
            pl.BlockSpec((1, RET_V), lambda bi, j: (0, 0)),
        ],
        out_specs=pl.BlockSpec((None, c, vw), lambda bi, j: (bi, fmap(j), 0)),
        out_shape=jax.ShapeDtypeStruct((b, n, vw), BF16),
        scratch_shapes=scratch,
        compiler_params=_cparams(("arbitrary", "arbitrary")),
        name="retention_fwd",
    )(p0, p0, p0, logit, o_b, p0, rnorm_w.reshape(1, RET_V))


def _wattn_kernel(q_ref, kc_ref, vc_ref, kp_ref, vp_ref, k0_ref, v0_ref, kn_ref, vn_ref, sink_ref, o_ref,
                  *, n_tiles, group):
    i = pl.program_id(1)
    tq = q_ref.shape[0]
    kcat = jnp.concatenate([kc_ref[...], kp_ref[...], k0_ref[...], kn_ref[...]], axis=0)
    vcat = jnp.concatenate([vc_ref[...], vp_ref[...], v0_ref[...], vn_ref[...]], axis=0)
    kk = lax.broadcasted_iota(jnp.int32, (4 * tq, tq), 0)
    qq = lax.broadcasted_iota(jnp.int32, (4 * tq, tq), 1)
    seg = kk // tq
    delta = qq - (kk - seg * tq) + (2 - seg) * tq
    far = 4 * tq
    miss = jnp.where(seg == 1, jnp.where(i > 0, 0, far), jnp.where(seg == 3, jnp.where(i < n_tiles - 1, 0, far), 0))
    dist = jnp.where(seg == 0, 0, jnp.abs(delta) + miss)
    bias1 = jnp.where(dist <= WINDOW, 0.0, NEG_INF).astype(F32)
    bias = jnp.concatenate([bias1, bias1], axis=1)
    lane = lax.broadcasted_iota(jnp.int32, (tq, LANES), 1)
    lo_half = lane < SW_DIM
    for hk in range(SW_KV_HEADS):
        kh = kcat[:, hk * SW_DIM:(hk + 1) * SW_DIM]
        kdup = jnp.concatenate([kh, kh], axis=1)
        for pg in range(group // 2):
            pair = hk * (group // 2) + pg
            qp = q_ref[:, pair * LANES:(pair + 1) * LANES]
            zero = jnp.zeros_like(qp)
            qstack = jnp.concatenate([jnp.where(lo_half, qp, zero), jnp.where(lo_half, zero, qp)], axis=0)
            s = _dot_nt(kdup, qstack) + bias
            sink = jnp.concatenate([jnp.broadcast_to(sink_ref[2 * pair][:, :1], (1, tq)),
                                    jnp.broadcast_to(sink_ref[2 * pair + 1][:, :1], (1, tq))], axis=1)
            m = jnp.maximum(jnp.max(s, axis=0, keepdims=True), sink)
            p = jnp.exp(s - m)
            den = jnp.sum(p, axis=0, keepdims=True) + jnp.exp(sink - m)
            ot = _dot_tn(vcat, p.astype(BF16)) * (1.0 / den)
            oth = ot[hk * SW_DIM:(hk + 1) * SW_DIM]
            both = jnp.concatenate([oth[:, :tq], oth[:, tq:]], axis=0)
            o_ref[:, pair * LANES:(pair + 1) * LANES] = both.T.astype(o_ref.dtype)


def window_attention(p1, sinks, n_ctx):
    b, n, _ = p1.shape
    tq = ROW_TILE
    assert n_ctx == tq and WINDOW <= tq
    n_tiles = (n - n_ctx) // tq
    group = SW_Q_HEADS // SW_KV_HEADS
    qw = SW_Q_HEADS * SW_DIM
    kvw = SW_KV_HEADS * SW_DIM
    assert kvw == LANES and group % 2 == 0
    k_col = qw // kvw
    v_col = k_col + 1
    sink_b = jnp.broadcast_to(sinks.astype(F32).reshape(SW_Q_HEADS, 1, 1), (SW_Q_HEADS, 1, LANES))

    def kv(col, tile):
        return pl.BlockSpec((None, tq, kvw), lambda bi, i: (bi, tile(i), col))

    prev = lambda i: jnp.maximum(i - 1, 0)
    cur = lambda i: i
    nxt = lambda i: jnp.minimum(i + 1, n_tiles - 1)
    ctx = lambda i: n_tiles
    return pl.pallas_call(
        functools.partial(_wattn_kernel, n_tiles=n_tiles, group=group),
        grid=(b, n_tiles),
        in_specs=[
            pl.BlockSpec((None, tq, qw), lambda bi, i: (bi, i, 0)),
            kv(k_col, ctx), kv(v_col, ctx), kv(k_col, prev), kv(v_col, prev),
            kv(k_col, cur), kv(v_col, cur), kv(k_col, nxt), kv(v_col, nxt),
            pl.BlockSpec((SW_Q_HEADS, 1, LANES), lambda bi, i: (0, 0, 0)),
        ],
        out_specs=pl.BlockSpec((None, tq, qw), lambda bi, i: (bi, i, 0)),
        out_shape=jax.ShapeDtypeStruct((b, n - n_ctx, qw), BF16),
        compiler_params=_cparams(("arbitrary", "arbitrary")),
        name="window_attn",
    )(p1, p1, p1, p1, p1, p1, p1, p1, p1, sink_b)


def _post_kernel(*refs, n_pieces):
    pieces = refs[:n_pieces]
    w_ref, h_ref, mod_ref, nw_ref, rw_ref, rb_ref, ho_ref, f_ref, lg_ref = refs[n_pieces:]
    m = None
    lo = 0
    for p in pieces:
        wdt = p.shape[1]
        t = _dot(p[...], w_ref[lo:lo + wdt, :])
        m = t if m is None else m + t
        lo += wdt
    md = mod_ref[...]
    hn = h_ref[...] + md[0:1] * m
    ho_ref[...] = hn
    f = _norm_mod(hn, nw_ref[...], md[1:2], md[2:3]).astype(BF16)
    f_ref[...] = f
    lg_ref[...] = _dot(f, rw_ref[...]) + rb_ref[...]


def post_mixer(pieces, w_out, h, mod, nw, router_w, router_b, n_lat_tiles):
    b, r, _ = pieces[0].shape
    d = h.shape[2]
    tm = ROW_TILE
    rw = jnp.zeros((d, LANES), BF16).at[:, :N_EXPERTS].set(router_w.astype(BF16))
    rb = jnp.full((1, LANES), NEG_INF, F32).at[0, :N_EXPERTS].set(router_b.astype(F32))
    row = lambda bi, i: (bi, i, 0)
    const2 = lambda bi, i: (0, 0)
    outs = pl.pallas_call(
        functools.partial(_post_kernel, n_pieces=len(pieces)),
        grid=(b, r // tm),
        in_specs=[pl.BlockSpec((None, tm, p.shape[2]), row) for p in pieces] + [
            pl.BlockSpec(w_out.shape, const2),
            pl.BlockSpec((None, tm, d), row),
            pl.BlockSpec((None, None, 3, d), lambda bi, i: (bi, jnp.where(i < n_lat_tiles, 1, 0), 0, 0)),
            pl.BlockSpec((1, d), const2),
            pl.BlockSpec((d, LANES), const2),
            pl.BlockSpec((1, LANES), const2),
        ],
        out_specs=[pl.BlockSpec((None, tm, d), row), pl.BlockSpec((None, tm, d), row),
                   pl.BlockSpec((None, tm, LANES), row)],
        out_shape=[jax.ShapeDtypeStruct((b, r, d), F32), jax.ShapeDtypeStruct((b, r, d), BF16),
                   jax.ShapeDtypeStruct((b, r, LANES), F32)],
        compiler_params=_cparams(("arbitrary", "arbitrary")),
        name="post_mixer",
    )(*pieces, w_out, h, mod, nw.reshape(1, d), rw, rb)
    return outs


def _ffn_kernel(be_ref, nb_ref, x_ref, w1_ref, b1_ref, w2_ref, b2_ref, o_ref):
    i = pl.program_id(0)

    @pl.when(i < nb_ref[0])
    def _():
        x = x_ref[...]
        dff = w2_ref.shape[0]
        acc = None
        for c in range(dff // FF_CHUNK):
            lo = c * FF_CHUNK
            hg = _dot(x, w1_ref[:, lo:lo + FF_CHUNK]) + b1_ref[:, lo:lo + FF_CHUNK]
            hl = _dot(x, w1_ref[:, dff + lo:dff + lo + FF_CHUNK]) + b1_ref[:, dff + lo:dff + lo + FF_CHUNK]
            glu = jnp.minimum(hg, SWIGLU_LIMIT)
            lin = jnp.clip(hl, -SWIGLU_LIMIT, SWIGLU_LIMIT)
            act = glu * _sigmoid(SWIGLU_ALPHA * glu) * (lin + 1.0)
            t = _dot(act.astype(BF16), w2_ref[lo:lo + FF_CHUNK, :])
            acc = t if acc is None else acc + t
        o_ref[...] = (acc + b2_ref[...]).astype(o_ref.dtype)

    @pl.when(i >= nb_ref[0])
    def _():
        o_ref[...] = jnp.zeros(o_ref.shape, o_ref.dtype)


def expert_ffn(x_buf, block_expert, n_used, w1, b1, w2, b2):
    p, d = x_buf.shape
    ne, _, f2 = w1.shape
    dff = w2.shape[1]
    tm = MOE_TILE
    grid_spec = pltpu.PrefetchScalarGridSpec(
        num_scalar_prefetch=2,
        grid=(p // tm,),
        in_specs=[
            pl.BlockSpec((tm, d), lambda i, be, nb: (i, 0)),
            pl.BlockSpec((None, d, f2), lambda i, be, nb: (be[i], 0, 0)),
            pl.BlockSpec((None, 1, f2), lambda i, be, nb: (be[i], 0, 0)),
            pl.BlockSpec((None, dff, d), lambda i, be, nb: (be[i], 0, 0)),
            pl.BlockSpec((None, 1, d), lambda i, be, nb: (be[i], 0, 0)),
        ],
        out_specs=pl.BlockSpec((tm, d), lambda i, be, nb: (i, 0)),
    )
    return pl.pallas_call(
        _ffn_kernel,
        grid_spec=grid_spec,
        out_shape=jax.ShapeDtypeStruct((p, d), BF16),
        compiler_params=_cparams(("arbitrary",)),
        name="expert_ffn",
    )(block_expert, n_used, x_buf, w1, b1.reshape(ne, 1, f2), w2, b2.reshape(ne, 1, d))


def route(logits):
    t = logits.shape[0]
    top_v, top_i = lax.top_k(logits[:, :N_EXPERTS], TOP_K)
    gate = jax.nn.softmax(top_v, axis=-1)
    m = t * TOP_K
    e_flat = top_i.reshape(m)
    onehot = (e_flat[:, None] == jnp.arange(N_EXPERTS, dtype=e_flat.dtype)[None, :]).astype(jnp.int32)
    csum = jnp.cumsum(onehot, axis=0)
    rank = jnp.take_along_axis(csum, e_flat[:, None], axis=1)[:, 0] - 1
    counts = csum[-1]
    padded = ((counts + MOE_TILE - 1) // MOE_TILE) * MOE_TILE
    pad_ends = jnp.cumsum(padded)
    pad_starts = pad_ends - padded
    dest = pad_starts[e_flat] + rank
    n_blocks = -(-m // MOE_TILE) + N_EXPERTS
    cap = n_blocks * MOE_TILE
    tok = jnp.repeat(jnp.arange(t, dtype=jnp.int32), TOP_K)
    src = jnp.zeros((cap,), jnp.int32).at[dest].set(tok)
    block_expert = jnp.minimum(
        jnp.searchsorted(pad_ends, jnp.arange(n_blocks, dtype=jnp.int32) * MOE_TILE, side='right'),
        N_EXPERTS - 1).astype(jnp.int32)
    n_used = (pad_ends[-1] // MOE_TILE).astype(jnp.int32).reshape(1)
    return gate, dest.reshape(t, TOP_K).astype(jnp.int32), src, block_expert, n_used


def _combine_kernel(*refs, final):
    if final:
        h_ref, y_ref, gt_ref, g2_ref, nw_ref, o_ref = refs
    else:
        h_ref, y_ref, gt_ref, g2_ref, o_ref = refs
    gt = gt_ref[...]
    acc = None
    for k in range(TOP_K):
        t = gt[:, k:k + 1] * y_ref[k].astype(F32)
        acc = t if acc is None else acc + t
    hn = h_ref[...] + g2_ref[...] * acc
    if final:
        y = hn * lax.rsqrt(jnp.mean(hn * hn, axis=-1, keepdims=True) + EPS)
        hn = y * nw_ref[...]
    o_ref[...] = hn


def combine(h, y_g, gate, g2, n_lat_tiles, final_w=None):
    b, r, d = h.shape
    tm = ROW_TILE
    final = final_w is not None
    in_specs = [
        pl.BlockSpec((None, tm, d), lambda bi, i: (bi, i, 0)),
        pl.BlockSpec((TOP_K, None, tm, d), lambda bi, i: (0, bi, i, 0)),
        pl.BlockSpec((None, tm, TOP_K), lambda bi, i: (bi, i, 0)),
        pl.BlockSpec((None, None, 1, d), lambda bi, i: (bi, jnp.where(i < n_lat_tiles, 1, 0), 0, 0)),
    ]
    args = [h, y_g, gate, g2]
    if final:
        in_specs.append(pl.BlockSpec((1, d), lambda bi, i: (0, 0)))
        args.append(final_w.reshape(1, d))
    return pl.pallas_call(
        functools.partial(_combine_kernel, final=final),
        grid=(b, r // tm),
        in_specs=in_specs,
        out_specs=pl.BlockSpec((None, tm, d), lambda bi, i: (bi, i, 0)),
        out_shape=jax.ShapeDtypeStruct((b, r, d), F32),
        compiler_params=_cparams(("arbitrary", "arbitrary")),
        name="combine",
    )(*args)


def moe_layer(h, f, logits, g2, n_lat_tiles, w1, b1, w2, b2, final_w=None):
    b, r, d = h.shape
    t = b * r
    gate, dest, src, block_expert, n_used = route(logits.reshape(t, LANES))
    x_buf = jnp.take(f.reshape(t, d), src, axis=0)
    y_buf = expert_ffn(x_buf, block_expert, n_used, w1, b1, w2, b2)
    y_g = jnp.take(y_buf, dest.T, axis=0).reshape(TOP_K, b, r, d)
    return combine(h, y_g, gate.reshape(b, r, TOP_K), g2, n_lat_tiles, final_w)


def _mod_table(rows, n_batch):
    lat = jnp.stack([r[:n_batch] for r in rows], axis=1)
    ctx = jnp.stack([r[n_batch] for r in rows], axis=0)
    return jnp.stack([jnp.broadcast_to(ctx[None], lat.shape), lat], axis=1)


def kernel(x, c, ctx, c_ctx, mod_w, mod_b, norm1_w, norm2_w, even_w_in, even_w_out, diff_lam, diff_norm_w,
           ret_decay_logit, ret_norm_w, odd_w_qkv, odd_w_out, odd_sinks, router_w, router_b,
           moe_w1, moe_b1, moe_w2, moe_b2, final_norm_w):
    b, s, d = x.shape
    n_ctx = ctx.shape[1]
    n_lat_tiles = s // ROW_TILE
    assert n_ctx == ROW_TILE and s % ROW_TILE == 0 and b + 1 <= 8

    cvec = jnp.zeros((8, d), F32).at[:b].set(c).at[b].set(c_ctx)
    mods = adaln_all(cvec, mod_w, mod_b)
    rope_tabs = rope_tables(s, n_ctx)
    w1 = moe_w1.astype(BF16)
    w2 = moe_w2.astype(BF16)

    sh1, sc1, g1, sh2, sc2, g2 = jnp.split(mods[0], 6, axis=-1)
    h = jnp.concatenate([x, ctx], axis=1)
    groups0 = ((True, 0.125),) * 4 + ((True, 1.0),) * 4 + ((False, 1.0),) * 4 \
        + ((False, 0.125),) * 2 + ((False, 1.0),) * 10
    p0 = proj_in(h, _mod_table([sh1, sc1], b), norm1_w[0], even_w_in[0].astype(BF16), rope_tabs, groups0,
                 n_lat_tiles)
    lam_init = 0.8 - 0.6 * math.exp(-0.3 * 0)
    d_out = diff_attention(p0, diff_lam[0], diff_norm_w[0], lam_init, n_ctx)
    r_out = retention(p0, ret_decay_logit[0], ret_norm_w[0])
    h, f, logits = post_mixer([d_out, r_out], even_w_out[0].astype(BF16), h, _mod_table([g1, sh2, sc2], b),
                              norm2_w[0], router_w[0], router_b[0], n_lat_tiles)
    h = moe_layer(h, f, logits, _mod_table([g2], b), n_lat_tiles, w1[0], moe_b1[0], w2[0], moe_b2[0])

    sh1, sc1, g1, sh2, sc2, g2 = jnp.split(mods[1], 6, axis=-1)
    groups1 = ((True, 0.125),) * 8 + ((True, 1.0),) + ((False, 1.0),)
    p1 = proj_in(h, _mod_table([sh1, sc1], b), norm1_w[1], odd_w_qkv[0].astype(BF16), rope_tabs, groups1,
                 n_lat_tiles)
    a_out = window_attention(p1, odd_sinks[0], n_ctx)
    h_lat, f, logits = post_mixer([a_out], odd_w_out[0].astype(BF16), h, _mod_table([g1, sh2, sc2], b),
                                  norm2_w[1], router_w[1], router_b[1], n_lat_tiles)
    return moe_layer(h_lat, f, logits, _mod_table([g2], b), n_lat_tiles, w1[1], moe_b1[1], w2[1], moe_b2[1],
                     final_w=final_norm_w)
```

```python
import functools
import math

import jax
import jax.numpy as jnp
from jax import lax
from jax.experimental import pallas as pl
from jax.experimental.pallas import tpu as pltpu

F32 = jnp.float32
BF16 = jnp.bfloat16

GRID_W = 64
EPS = 1e-6
ROPE_BASE = 10000.0
NEG_INF = -1e30

DA_HEADS = 4
DA_DIM = 64
RET_HEADS = 4
RET_QK = 64
RET_V = 128
SW_Q_HEADS = 16
SW_KV_HEADS = 2
SW_DIM = 64
WINDOW = 128
N_EXPERTS = 32
TOP_K = 4
SWIGLU_ALPHA = 1.702
SWIGLU_LIMIT = 7.0

LANES = 128
ROW_TILE = 256
MOE_TILE = 256
DATTN_TQ = 512
DATTN_TK = 768
FF_CHUNK = 512
VMEM_LIMIT = 48 * 1024 * 1024
FFN_VMEM_LIMIT = 56 * 1024 * 1024

E_DQ, E_DK, E_DV, E_RQ, E_RK, E_RV, E_RG = 0, 4, 8, 12, 14, 16, 20


def _cparams(sem):
    return pltpu.CompilerParams(dimension_semantics=sem, vmem_limit_bytes=VMEM_LIMIT)


def _dot(a, b):
    return jnp.dot(a, b, preferred_element_type=F32)


def _dot_nt(a, b):
    return lax.dot_general(a, b, (((1,), (1,)), ((), ())), preferred_element_type=F32)


def _dot_tn(a, b):
    return lax.dot_general(a, b, (((0,), (0,)), ((), ())), preferred_element_type=F32)


def _sigmoid(x):
    return 1.0 / (1.0 + jnp.exp(-x))


def _pack_pairs(x):
    w = x.shape[1] // 2
    lo = lax.bitcast_convert_type(x[:, :w].astype(BF16).astype(F32), jnp.uint32)
    hi = lax.bitcast_convert_type(x[:, w:].astype(BF16).astype(F32), jnp.uint32)
    return (hi & jnp.uint32(0xFFFF0000)) | (lo >> 16)


def _unpack_pairs(u):
    lo = lax.bitcast_convert_type(u << 16, F32)
    hi = lax.bitcast_convert_type(u & jnp.uint32(0xFFFF0000), F32)
    return jnp.concatenate([lo, hi], axis=1)


def _adaln_kernel(c_ref, w_ref, b_ref, o_ref):
    c = c_ref[...]
    s = c * _sigmoid(c)
    o_ref[...] = _dot(s.astype(BF16), w_ref[...].astype(BF16)) + b_ref[...]


def adaln_all(cvec, mod_w, mod_b):
    depth, d, d6 = mod_w.shape
    tn = 1024
    return pl.pallas_call(
        _adaln_kernel,
        grid=(depth, d6 // tn),
        in_specs=[
            pl.BlockSpec((8, d), lambda l, j: (0, 0)),
            pl.BlockSpec((None, d, tn), lambda l, j: (l, 0, j)),
            pl.BlockSpec((None, 1, tn), lambda l, j: (l, 0, j)),
        ],
        out_specs=pl.BlockSpec((None, 8, tn), lambda l, j: (l, 0, j)),
        out_shape=jax.ShapeDtypeStruct((depth, 8, d6), F32),
        compiler_params=_cparams(("arbitrary", "arbitrary")),
        name="adaln",
    )(cvec, mod_w, mod_b.reshape(depth, 1, d6))


def _norm_mod(x, nw, shift, scale):
    y = x * lax.rsqrt(jnp.mean(x * x, axis=-1, keepdims=True) + EPS)
    return (y * nw) * (1.0 + scale) + shift


def _proj_kernel(h_ref, mod_ref, nw_ref, w_ref, cos_ref, sa_ref, sb_ref, o_ref, *, groups, chunk):
    m = mod_ref[...]
    a = _norm_mod(h_ref[...], nw_ref[...], m[0:1], m[1:2]).astype(BF16)
    cos = cos_ref[...]
    sa = sa_ref[...]
    sb = sb_ref[...]
    gpc = chunk // LANES
    for c in range(len(groups) // gpc):
        acc = _dot(a, w_ref[:, c * chunk:(c + 1) * chunk])
        for gi in range(gpc):
            rope, scale = groups[c * gpc + gi]
            blk = acc[:, gi * LANES:(gi + 1) * LANES]
            if rope:
                blk = blk * cos + pltpu.roll(blk, LANES - 16, 1) * sa + pltpu.roll(blk, 16, 1) * sb
            if scale != 1.0:
                blk = blk * scale
            lo = (c * gpc + gi) * LANES
            o_ref[:, lo:lo + LANES] = blk.astype(BF16)


def proj_in(h, mod, nw, w, rope_tabs, groups, n_lat_tiles):
    b, n, d = h.shape
    dout = w.shape[1]
    tm = ROW_TILE
    chunk = 512 if dout % 512 == 0 else 256
    cos, sa, sb = rope_tabs
    tab_spec = pl.BlockSpec((tm, LANES), lambda bi, i: (i, 0))
    return pl.pallas_call(
        functools.partial(_proj_kernel, groups=groups, chunk=chunk),
        grid=(b, n // tm),
        in_specs=[
            pl.BlockSpec((None, tm, d), lambda bi, i: (bi, i, 0)),
            pl.BlockSpec((None, None, 2, d), lambda bi, i: (bi, jnp.where(i < n_lat_tiles, 1, 0), 0, 0)),
            pl.BlockSpec((1, d), lambda bi, i: (0, 0)),
            pl.BlockSpec((d, dout), lambda bi, i: (0, 0)),
            tab_spec, tab_spec, tab_spec,
        ],
        out_specs=pl.BlockSpec((None, tm, dout), lambda bi, i: (bi, i, 0)),
        out_shape=jax.ShapeDtypeStruct((b, n, dout), BF16),
        compiler_params=_cparams(("arbitrary", "arbitrary")),
        name="proj_in",
    )(h, mod, nw.reshape(1, d), w, cos, sa, sb)


def rope_tables(n_lat, n_ctx):
    f = DA_DIM // 4
    inv = ROPE_BASE ** (-jnp.arange(f, dtype=F32) / f)
    t = jnp.arange(n_lat)
    row = (t // GRID_W).astype(F32)
    col = (t % GRID_W).astype(F32)
    ang_r = row[:, None] * inv[None, :]
    ang_c = col[:, None] * inv[None, :]
    ang = jnp.concatenate([ang_r, ang_r, ang_c, ang_c], axis=-1)
    ang = jnp.concatenate([ang, jnp.zeros((n_ctx, 4 * f), F32)], axis=0)
    cos = jnp.cos(ang)
    sin = jnp.sin(ang)
    first = (jnp.arange(4 * f) % (2 * f)) < f
    sa = jnp.where(first[None, :], -sin, 0.0)
    sb = jnp.where(first[None, :], 0.0, sin)
    rep = LANES // (4 * f)
    return tuple(jnp.tile(x, (1, rep)) for x in (cos, sa, sb))


def _dattn_kernel(q_ref, k_ref, v_ref, lam_ref, nw_ref, o_ref, acc_sc, *, tk, lam_init):
    tq = q_ref.shape[0]
    n_chunks = k_ref.shape[0] // tk
    q = q_ref[...]
    lane = lax.broadcasted_iota(jnp.int32, q.shape, 1)
    zero = jnp.zeros_like(q)
    qs = (jnp.where(lane < DA_DIM, q, zero), jnp.where(lane >= DA_DIM, q, zero))
    acc_sc[...] = jnp.zeros(acc_sc.shape, F32)

    def body(c, carry):
        ms, ls = carry
        off = pl.multiple_of(c * tk, tk)
        k = k_ref[pl.ds(off, tk), :]
        v = v_ref[pl.ds(off, tk), :]
        new_m, new_l = [], []
        for sub in range(2):
            s = _dot_nt(k, qs[sub])
            m_new = jnp.maximum(ms[sub], jnp.max(s, axis=0, keepdims=True))
            alpha = jnp.exp(ms[sub] - m_new)
            p = jnp.exp(s - m_new)
            new_l.append(alpha * ls[sub] + jnp.sum(p, axis=0, keepdims=True))
            acc_sc[sub] = alpha * acc_sc[sub] + _dot_tn(v, p.astype(BF16))
            new_m.append(m_new)
        return tuple(new_m), tuple(new_l)

    m0 = jnp.full((1, tq), -jnp.inf, F32)
    l0 = jnp.zeros((1, tq), F32)
    _, ls = lax.fori_loop(0, n_chunks, body, ((m0, m0), (l0, l0)))

    lp = lam_ref[...]
    lam = (jnp.exp(jnp.sum(lp[0:1] * lp[1:2], axis=-1, keepdims=True))
           - jnp.exp(jnp.sum(lp[2:3] * lp[3:4], axis=-1, keepdims=True))) + lam_init
    o = acc_sc[0] * (1.0 / ls[0]) - lam * (acc_sc[1] * (1.0 / ls[1]))
    y = o * lax.rsqrt(jnp.mean(o * o, axis=0, keepdims=True) + EPS)
    y = (y * nw_ref[...]) * (1.0 - lam_init)
    o_ref[...] = y.T.astype(o_ref.dtype)


def _dattn_call(p0, lam_p, dnorm_w, lam_init, tq, tk, q_tiles, q_tile0, kv_rows, kv_tile0):
    b = p0.shape[0]
    hd = 2 * DA_DIM
    return pl.pallas_call(
        functools.partial(_dattn_kernel, tk=tk, lam_init=lam_init),
        grid=(b, DA_HEADS, q_tiles),
        in_specs=[
            pl.BlockSpec((None, tq, hd), lambda bi, h, i: (bi, i + q_tile0, E_DQ + h)),
            pl.BlockSpec((None, kv_rows, hd), lambda bi, h, i: (bi, kv_tile0, E_DK + h)),
            pl.BlockSpec((None, kv_rows, hd), lambda bi, h, i: (bi, kv_tile0, E_DV + h)),
            pl.BlockSpec((4, DA_DIM), lambda bi, h, i: (0, 0)),
            pl.BlockSpec((hd, 1), lambda bi, h, i: (0, 0)),
        ],
        out_specs=pl.BlockSpec((None, tq, hd), lambda bi, h, i: (bi, i, h)),
        out_shape=jax.ShapeDtypeStruct((b, q_tiles * tq, DA_HEADS * hd), BF16),
        scratch_shapes=[pltpu.VMEM((2, hd, tq), F32)],
        compiler_params=_cparams(("arbitrary", "arbitrary", "arbitrary")),
        name="diff_attn",
    )(p0, p0, p0, lam_p, dnorm_w.reshape(hd, 1))


def diff_attention(p0, lam_p, dnorm_w, lam_init, n_ctx):
    n = p0.shape[1]
    s = n - n_ctx
    tq = min(DATTN_TQ, s)
    tk = DATTN_TK if n % DATTN_TK == 0 else n_ctx
    assert s % tq == 0 and n % tk == 0 and s % n_ctx == 0
    lat = _dattn_call(p0, lam_p, dnorm_w, lam_init, tq, tk, s // tq, 0, n, 0)
    ctx = _dattn_call(p0, lam_p, dnorm_w, lam_init, n_ctx, n_ctx, 1, s // n_ctx, n_ctx, s // n_ctx)
    return jnp.concatenate([lat, ctx], axis=1)


def _ret_kernel(*refs, reverse, finalize):
    if finalize:
        q_ref, k_ref, v_ref, lg_ref, ob_ref, g_ref, nw_ref, o_ref, s_sc = refs
    else:
        q_ref, k_ref, v_ref, lg_ref, o_ref, s_sc = refs
    j = pl.program_id(1)
    c = q_ref.shape[0]

    @pl.when(j == 0)
    def _():
        s_sc[...] = jnp.zeros(s_sc.shape, F32)

    ri = lax.broadcasted_iota(jnp.int32, (c, c), 0)
    ci = lax.broadcasted_iota(jnp.int32, (c, c), 1)
    dist = (ci - ri) if reverse else (ri - ci)
    distf = jnp.maximum(dist, 0).astype(F32)
    pos = lax.broadcasted_iota(jnp.int32, (c, RET_QK), 0).astype(F32)
    q_exp = (c - pos) if reverse else (pos + 1.0)
    k_exp = pos if reverse else (c - 1.0 - pos)

    for h in range(RET_HEADS):
        x = lg_ref[h]
        lg = -jnp.log(1.0 + jnp.exp(-x))
        lgq = lg[:, :RET_QK]
        q = q_ref[:, h * RET_QK:(h + 1) * RET_QK].astype(F32)
        k = k_ref[:, h * RET_QK:(h + 1) * RET_QK].astype(F32)
        v = v_ref[:, h * RET_V:(h + 1) * RET_V]
        decay = jnp.where(dist >= 0, jnp.exp(lg[:, :1] * distf), 0.0)
        scores = _dot_nt(q.astype(BF16), k.astype(BF16)) * decay
        inner = _dot(scores.astype(BF16), v)
        q_dec = q * jnp.exp(lgq * q_exp)
        k_dec = k * jnp.exp(lgq * k_exp)
        state = s_sc[h]
        out = inner + _dot(q_dec.astype(BF16), state.astype(BF16))
        s_sc[h] = jnp.exp(lg * float(c)) * state + _dot_tn(k_dec.astype(BF16), v)
        if finalize:
            r = out + ob_ref[:, h * RET_V:(h + 1) * RET_V]
            y = r * lax.rsqrt(jnp.mean(r * r, axis=-1, keepdims=True) + EPS) * nw_ref[...]
            g = g_ref[:, h * RET_V:(h + 1) * RET_V].astype(F32)
            o_ref[:, h * RET_V:(h + 1) * RET_V] = (y * (g * _sigmoid(g))).astype(o_ref.dtype)
        else:
            o_ref[:, h * RET_V:(h + 1) * RET_V] = out


def retention(p0, decay_logit, rnorm_w):
    b, n, _ = p0.shape
    c = ROW_TILE
    nc = n // c
    logit = jnp.broadcast_to(decay_logit.astype(F32)[:, :, None, None], (2, RET_HEADS, 1, LANES))
    qw, vw = RET_HEADS * RET_QK, RET_HEADS * RET_V

    def specs(cmap):
        return [
            pl.BlockSpec((None, c, qw), lambda bi, j: (bi, cmap(j), E_RQ * LANES // qw)),
            pl.BlockSpec((None, c, qw), lambda bi, j: (bi, cmap(j), E_RK * LANES // qw)),
            pl.BlockSpec((None, c, vw), lambda bi, j: (bi, cmap(j), E_RV * LANES // vw)),
        ]

    lg_spec = lambda d: pl.BlockSpec((None, RET_HEADS, 1, LANES), lambda bi, j: (d, 0, 0, 0))
    scratch = [pltpu.VMEM((RET_HEADS, RET_QK, RET_V), F32)]
    bmap = lambda j: jnp.where(j == 0, nc - 1, nc - 1 - j)
    o_b = pl.pallas_call(
        functools.partial(_ret_kernel, reverse=True, finalize=False),
        grid=(b, nc),
        in_specs=specs(bmap) + [lg_spec(1)],
        out_specs=pl.BlockSpec((None, c, vw), lambda bi, j: (bi, bmap(j), 0)),
        out_shape=jax.ShapeDtypeStruct((b, n, vw), F32),
        scratch_shapes=scratch,
        compiler_params=_cparams(("arbitrary", "arbitrary")),
        name="retention_bwd",
    )(p0, p0, p0, logit)
    fmap = lambda j: jnp.where(j == 0, nc - 1, j - 1)
    return pl.pallas_call(
        functools.partial(_ret_kernel, reverse=False, finalize=True),
        grid=(b, nc),
        in_specs=specs(fmap) + [
            lg_spec(0),
            pl.BlockSpec((None, c, vw), lambda bi, j: (bi, fmap(j), 0)),
            pl.BlockSpec((None, c, vw), lambda bi, j: (bi, fmap(j), E_RG * LANES // vw)),
            pl.BlockSpec((1, RET_V), lambda bi, j: (0, 0)),
        ],
        out_specs=pl.BlockSpec((None, c, vw), lambda bi, j: (bi, fmap(j), 0)),
        out_shape=jax.ShapeDtypeStruct((b, n, vw), BF16),
        scratch_shapes=scratch,
        compiler_params=_cparams(("arbitrary", "arbitrary")),
        name="retention_fwd",
    )(p0, p0, p0, logit, o_b, p0, rnorm_w.reshape(1, RET_V))


def _wattn_kernel(q_ref, kc_ref, vc_ref, kp_ref, vp_ref, k0_ref, v0_ref, kn_ref, vn_ref, sink_ref, o_ref,
                  *, n_tiles, group):
    i = pl.program_id(1)
    tq = q_ref.shape[0]
    kcat = jnp.concatenate([kc_ref[...], kp_ref[...], k0_ref[...], kn_ref[...]], axis=0)
    vcat = jnp.concatenate([vc_ref[...], vp_ref[...], v0_ref[...], vn_ref[...]], axis=0)
    kk = lax.broadcasted_iota(jnp.int32, (4 * tq, tq), 0)
    qq = lax.broadcasted_iota(jnp.int32, (4 * tq, tq), 1)
    seg = kk // tq
    delta = qq - (kk - seg * tq) + (2 - seg) * tq
    far = 4 * tq
    miss = jnp.where(seg == 1, jnp.where(i > 0, 0, far), jnp.where(seg == 3, jnp.where(i < n_tiles - 1, 0, far), 0))
    dist = jnp.where(seg == 0, 0, jnp.abs(delta) + miss)
    bias1 = jnp.where(dist <= WINDOW, 0.0, NEG_INF).astype(F32)
    bias = jnp.concatenate([bias1, bias1], axis=1)
    lane = lax.broadcasted_iota(jnp.int32, (tq, LANES), 1)
    lo_half = lane < SW_DIM
    for hk in range(SW_KV_HEADS):
        kh = kcat[:, hk * SW_DIM:(hk + 1) * SW_DIM]
        kdup = jnp.concatenate([kh, kh], axis=1)
        for pg in range(group // 2):
            pair = hk * (group // 2) + pg
            qp = q_ref[:, pair * LANES:(pair + 1) * LANES]
            zero = jnp.zeros_like(qp)
            qstack = jnp.concatenate([jnp.where(lo_half, qp, zero), jnp.where(lo_half, zero, qp)], axis=0)
            s = _dot_nt(kdup, qstack) + bias
            sink = jnp.concatenate([jnp.broadcast_to(sink_ref[2 * pair][:, :1], (1, tq)),
                                    jnp.broadcast_to(sink_ref[2 * pair + 1][:, :1], (1, tq))], axis=1)
            m = jnp.maximum(jnp.max(s, axis=0, keepdims=True), sink)
            p = jnp.exp(s - m)
            den = jnp.sum(p, axis=0, keepdims=True) + jnp.exp(sink - m)
            ot = _dot_tn(vcat, p.astype(BF16)) * (1.0 / den)
            oth = ot[hk * SW_DIM:(hk + 1) * SW_DIM]
            both = jnp.concatenate([oth[:, :tq], oth[:, tq:]], axis=0)
            o_ref[:, pair * LANES:(pair + 1) * LANES] = both.T.astype(o_ref.dtype)


def window_attention(p1, sinks, n_ctx):
    b, n, _ = p1.shape
    tq = ROW_TILE
    assert n_ctx == tq and WINDOW <= tq
    n_tiles = (n - n_ctx) // tq
    group = SW_Q_HEADS // SW_KV_HEADS
    qw = SW_Q_HEADS * SW_DIM
    kvw = SW_KV_HEADS * SW_DIM
    assert kvw == LANES and group % 2 == 0
    k_col = qw // kvw
    v_col = k_col + 1
    sink_b = jnp.broadcast_to(sinks.astype(F32).reshape(SW_Q_HEADS, 1, 1), (SW_Q_HEADS, 1, LANES))

    def kv(col, tile):
        return pl.BlockSpec((None, tq, kvw), lambda bi, i: (bi, tile(i), col))

    prev = lambda i: jnp.maximum(i - 1, 0)
    cur = lambda i: i
    nxt = lambda i: jnp.minimum(i + 1, n_tiles - 1)
    ctx = lambda i: n_tiles
    return pl.pallas_call(
        functools.partial(_wattn_kernel, n_tiles=n_tiles, group=group),
        grid=(b, n_tiles),
        in_specs=[
            pl.BlockSpec((None, tq, qw), lambda bi, i: (bi, i, 0)),
            kv(k_col, ctx), kv(v_col, ctx), kv(k_col, prev), kv(v_col, prev),
            kv(k_col, cur), kv(v_col, cur), kv(k_col, nxt), kv(v_col, nxt),
            pl.BlockSpec((SW_Q_HEADS, 1, LANES), lambda bi, i: (0, 0, 0)),
        ],
        out_specs=pl.BlockSpec((None, tq, qw), lambda bi, i: (bi, i, 0)),
        out_shape=jax.ShapeDtypeStruct((b, n - n_ctx, qw), BF16),
        compiler_params=_cparams(("arbitrary", "arbitrary")),
        name="window_attn",
    )(p1, p1, p1, p1, p1, p1, p1, p1, p1, sink_b)


def _post_kernel(*refs, n_pieces):
    pieces = refs[:n_pieces]
    w_ref, h_ref, mod_ref, nw_ref, rw_ref, rb_ref, ho_ref, f_ref, lg_ref = refs[n_pieces:]
    m = None
    lo = 0
    for p in pieces:
        wdt = p.shape[1]
        t = _dot(p[...], w_ref[lo:lo + wdt, :])
        m = t if m is None else m + t
        lo += wdt
    md = mod_ref[...]
    hn = h_ref[...] + md[0:1] * m
    ho_ref[...] = hn
    f = _norm_mod(hn, nw_ref[...], md[1:2], md[2:3])
    f_ref[...] = _pack_pairs(f)
    lg_ref[...] = _dot(f.astype(BF16), rw_ref[...]) + rb_ref[...]


def post_mixer(pieces, w_out, h, mod, nw, router_w, router_b, n_lat_tiles):
    b, r, _ = pieces[0].shape
    d = h.shape[2]
    tm = ROW_TILE
    rw = jnp.zeros((d, LANES), BF16).at[:, :N_EXPERTS].set(router_w.astype(BF16))
    rb = jnp.full((1, LANES), NEG_INF, F32).at[0, :N_EXPERTS].set(router_b.astype(F32))
    row = lambda bi, i: (bi, i, 0)
    const2 = lambda bi, i: (0, 0)
    outs = pl.pallas_call(
        functools.partial(_post_kernel, n_pieces=len(pieces)),
        grid=(b, r // tm),
        in_specs=[pl.BlockSpec((None, tm, p.shape[2]), row) for p in pieces] + [
            pl.BlockSpec(w_out.shape, const2),
            pl.BlockSpec((None, tm, d), row),
            pl.BlockSpec((None, None, 3, d), lambda bi, i: (bi, jnp.where(i < n_lat_tiles, 1, 0), 0, 0)),
            pl.BlockSpec((1, d), const2),
            pl.BlockSpec((d, LANES), const2),
            pl.BlockSpec((1, LANES), const2),
        ],
        out_specs=[pl.BlockSpec((None, tm, d), row), pl.BlockSpec((None, tm, d // 2), row),
                   pl.BlockSpec((None, tm, LANES), row)],
        out_shape=[jax.ShapeDtypeStruct((b, r, d), F32), jax.ShapeDtypeStruct((b, r, d // 2), jnp.uint32),
                   jax.ShapeDtypeStruct((b, r, LANES), F32)],
        compiler_params=_cparams(("arbitrary", "arbitrary")),
        name="post_mixer",
    )(*pieces, w_out, h, mod, nw.reshape(1, d), rw, rb)
    return outs


def _router_kernel(lg_ref, info_ref, cnt_ref, base_sc):
    i = pl.program_id(0)
    tm = lg_ref.shape[0]

    @pl.when(i == 0)
    def _():
        base_sc[...] = jnp.zeros(base_sc.shape, F32)

    l = lg_ref[...]
    lane = lax.broadcasted_iota(jnp.int32, l.shape, 1)
    lanef = lane.astype(F32)
    hots, vals, idxs = [], [], []
    for _ in range(TOP_K):
        m = jnp.max(l, axis=1, keepdims=True)
        idx = jnp.min(jnp.where(l == m, lanef, float(LANES)), axis=1, keepdims=True)
        hot = lanef == idx
        hots.append(hot)
        vals.append(m)
        idxs.append(idx)
        l = jnp.where(hot, -jnp.inf, l)
    es = [jnp.exp(v - vals[0]) for v in vals]
    tot = es[0] + es[1] + es[2] + es[3]
    sel = jnp.zeros(l.shape, F32)
    for hot in hots:
        sel = sel + jnp.where(hot, 1.0, 0.0)
    ri = lax.broadcasted_iota(jnp.int32, (tm, tm), 0)
    ci = lax.broadcasted_iota(jnp.int32, (tm, tm), 1)
    earlier = jnp.where(ri > ci, 1.0, 0.0).astype(BF16)
    before = _dot(earlier, sel.astype(BF16)) + base_sc[...]
    info = jnp.zeros(l.shape, F32)
    for k in range(TOP_K):
        pos = jnp.sum(jnp.where(hots[k], before, 0.0), axis=1, keepdims=True)
        info = jnp.where(lane == k, idxs[k], info)
        info = jnp.where(lane == TOP_K + k, pos, info)
        info = jnp.where(lane == 2 * TOP_K + k, es[k] / tot, info)
    info_ref[...] = info
    base_sc[...] = base_sc[...] + jnp.sum(sel, axis=0, keepdims=True)
    cnt_ref[...] = jnp.broadcast_to(base_sc[...], cnt_ref.shape)


def route(logits):
    t = logits.shape[0]
    tm = ROW_TILE
    info, cnt = pl.pallas_call(
        _router_kernel,
        grid=(t // tm,),
        in_specs=[pl.BlockSpec((tm, LANES), lambda i: (i, 0))],
        out_specs=[pl.BlockSpec((tm, LANES), lambda i: (i, 0)), pl.BlockSpec((8, LANES), lambda i: (0, 0))],
        out_shape=[jax.ShapeDtypeStruct((t, LANES), F32), jax.ShapeDtypeStruct((8, LANES), F32)],
        scratch_shapes=[pltpu.VMEM((1, LANES), F32)],
        compiler_params=_cparams(("arbitrary",)),
        name="router",
    )(logits)
    counts = cnt[0, :N_EXPERTS].astype(jnp.int32)
    padded = ((counts + MOE_TILE - 1) // MOE_TILE) * MOE_TILE
    pad_ends = jnp.cumsum(padded)
    pad_starts = pad_ends - padded
    e = info[:, :TOP_K].astype(jnp.int32)
    pos = info[:, TOP_K:2 * TOP_K].astype(jnp.int32)
    dest = jnp.take(pad_starts, e) + pos
    n_blocks = -(-(t * TOP_K) // MOE_TILE) + N_EXPERTS
    block_expert = jnp.minimum(
        jnp.searchsorted(pad_ends, jnp.arange(n_blocks, dtype=jnp.int32) * MOE_TILE, side='right'),
        N_EXPERTS - 1).astype(jnp.int32)
    n_used = (pad_ends[-1] // MOE_TILE).astype(jnp.int32).reshape(1)
    dest_tiles = dest.astype(jnp.int32).reshape(t // tm, 1, tm * TOP_K)
    return info, dest_tiles, block_expert, n_used, n_blocks


def _row_copies(t, dest_ref, make):
    return [make(t, k, dest_ref[0, t * TOP_K + k]) for k in range(TOP_K)]


def _dispatch_kernel(dest_ref, f_ref, xin_ref, xbuf_ref, sem):
    del xin_ref
    tm = f_ref.shape[0]

    def make(t, k, d):
        return pltpu.make_async_copy(f_ref.at[pl.ds(t, 1), :], xbuf_ref.at[pl.ds(d, 1), :], sem)

    def start(t, carry):
        for cp in _row_copies(t, dest_ref, make):
            cp.start()
        return carry

    def wait(t, carry):
        for cp in _row_copies(t, dest_ref, make):
            cp.wait()
        return carry

    lax.fori_loop(0, tm, start, 0)
    lax.fori_loop(0, tm, wait, 0)


def dispatch(f_packed, dest_tiles, n_slots):
    t, w = f_packed.shape
    tm = ROW_TILE
    x0 = jnp.zeros((n_slots, w), jnp.uint32)
    return pl.pallas_call(
        _dispatch_kernel,
        grid=(t // tm,),
        in_specs=[
            pl.BlockSpec((None, 1, tm * TOP_K), lambda i: (i, 0, 0), memory_space=pltpu.SMEM),
            pl.BlockSpec((tm, w), lambda i: (i, 0)),
            pl.BlockSpec(memory_space=pl.ANY),
        ],
        out_specs=pl.BlockSpec(memory_space=pl.ANY),
        out_shape=jax.ShapeDtypeStruct((n_slots, w), jnp.uint32),
        scratch_shapes=[pltpu.SemaphoreType.DMA(())],
        input_output_aliases={2: 0},
        compiler_params=_cparams(("arbitrary",)),
        name="dispatch",
    )(dest_tiles, f_packed, x0)


def _ffn_kernel(be_ref, nb_ref, x_ref, w1_ref, b1_ref, w2_ref, b2_ref, o_ref, w1_sc, w2_sc):
    i = pl.program_id(0)
    used = i < nb_ref[0]
    fresh = jnp.logical_or(i == 0, be_ref[i] != be_ref[jnp.maximum(i - 1, 0)])

    @pl.when(jnp.logical_and(used, fresh))
    def _():
        w1_sc[...] = w1_ref[...].astype(BF16)
        w2_sc[...] = w2_ref[...].astype(BF16)

    @pl.when(used)
    def _():
        x = _unpack_pairs(x_ref[...]).astype(BF16)
        dff = w2_sc.shape[0]
        acc = None
        for c in range(dff // FF_CHUNK):
            lo = c * FF_CHUNK
            hg = _dot(x, w1_sc[:, lo:lo + FF_CHUNK]) + b1_ref[:, lo:lo + FF_CHUNK]
            hl = _dot(x, w1_sc[:, dff + lo:dff + lo + FF_CHUNK]) + b1_ref[:, dff + lo:dff + lo + FF_CHUNK]
            glu = jnp.minimum(hg, SWIGLU_LIMIT)
            lin = jnp.clip(hl, -SWIGLU_LIMIT, SWIGLU_LIMIT)
            act = glu * _sigmoid(SWIGLU_ALPHA * glu) * (lin + 1.0)
            t = _dot(act.astype(BF16), w2_sc[lo:lo + FF_CHUNK, :])
            acc = t if acc is None else acc + t
        o_ref[...] = _pack_pairs(acc + b2_ref[...])

    @pl.when(jnp.logical_not(used))
    def _():
        o_ref[...] = jnp.zeros(o_ref.shape, o_ref.dtype)


def expert_ffn(x_buf, block_expert, n_used, w1, b1, w2, b2):
    p, w = x_buf.shape
    ne, d, f2 = w1.shape
    dff = w2.shape[1]
    tm = MOE_TILE
    grid_spec = pltpu.PrefetchScalarGridSpec(
        num_scalar_prefetch=2,
        grid=(p // tm,),
        in_specs=[
            pl.BlockSpec((tm, w), lambda i, be, nb: (i, 0)),
            pl.BlockSpec((None, d, f2), lambda i, be, nb: (be[i], 0, 0)),
            pl.BlockSpec((None, 1, f2), lambda i, be, nb: (be[i], 0, 0)),
            pl.BlockSpec((None, dff, d), lambda i, be, nb: (be[i], 0, 0)),
            pl.BlockSpec((None, 1, d), lambda i, be, nb: (be[i], 0, 0)),
        ],
        out_specs=pl.BlockSpec((tm, w), lambda i, be, nb: (i, 0)),
        scratch_shapes=[pltpu.VMEM((d, f2), BF16), pltpu.VMEM((dff, d), BF16)],
    )
    return pl.pallas_call(
        _ffn_kernel,
        grid_spec=grid_spec,
        out_shape=jax.ShapeDtypeStruct((p, w), jnp.uint32),
        compiler_params=pltpu.CompilerParams(dimension_semantics=("arbitrary",), vmem_limit_bytes=FFN_VMEM_LIMIT),
        name="expert_ffn",
    )(block_expert, n_used, x_buf, w1, b1.reshape(ne, 1, f2), w2, b2.reshape(ne, 1, d))


def _combine_kernel(*refs, final):
    if final:
        dest_ref, h_ref, info_ref, g2_ref, nw_ref, ybuf_ref, o_ref, gbuf, sem = refs
    else:
        dest_ref, h_ref, info_ref, g2_ref, ybuf_ref, o_ref, gbuf, sem = refs
    tm = h_ref.shape[0]

    def make(t, k, d):
        return pltpu.make_async_copy(ybuf_ref.at[pl.ds(d, 1), :], gbuf.at[k, pl.ds(t, 1), :], sem)

    def start(t, carry):
        for cp in _row_copies(t, dest_ref, make):
            cp.start()
        return carry

    def wait(t, carry):
        for cp in _row_copies(t, dest_ref, make):
            cp.wait()
        return carry

    lax.fori_loop(0, tm, start, 0)
    lax.fori_loop(0, tm, wait, 0)

    info = info_ref[...]
    acc = None
    for k in range(TOP_K):
        t = info[:, 2 * TOP_K + k:2 * TOP_K + k + 1] * _unpack_pairs(gbuf[k])
        acc = t if acc is None else acc + t
    hn = h_ref[...] + g2_ref[...] * acc
    if final:
        y = hn * lax.rsqrt(jnp.mean(hn * hn, axis=-1, keepdims=True) + EPS)
        hn = y * nw_ref[...]
    o_ref[...] = hn


def combine(h, y_buf, info, dest_tiles, g2, n_lat_tiles, final_w=None):
    b, r, d = h.shape
    tm = ROW_TILE
    nt = r // tm
    final = final_w is not None
    in_specs = [
        pl.BlockSpec((None, 1, tm * TOP_K), lambda bi, i: (bi * nt + i, 0, 0), memory_space=pltpu.SMEM),
        pl.BlockSpec((None, tm, d), lambda bi, i: (bi, i, 0)),
        pl.BlockSpec((None, tm, LANES), lambda bi, i: (bi, i, 0)),
        pl.BlockSpec((None, None, 1, d), lambda bi, i: (bi, jnp.where(i < n_lat_tiles, 1, 0), 0, 0)),
    ]
    args = [dest_tiles, h, info, g2]
    if final:
        in_specs.append(pl.BlockSpec((1, d), lambda bi, i: (0, 0)))
        args.append(final_w.reshape(1, d))
    in_specs.append(pl.BlockSpec(memory_space=pl.ANY))
    args.append(y_buf)
    return pl.pallas_call(
        functools.partial(_combine_kernel, final=final),
        grid=(b, nt),
        in_specs=in_specs,
        out_specs=pl.BlockSpec((None, tm, d), lambda bi, i: (bi, i, 0)),
        out_shape=jax.ShapeDtypeStruct((b, r, d), F32),
        scratch_shapes=[pltpu.VMEM((TOP_K, tm, d // 2), jnp.uint32), pltpu.SemaphoreType.DMA(())],
        compiler_params=_cparams(("arbitrary", "arbitrary")),
        name="combine",
    )(*args)


def moe_layer(h, f_packed, logits, g2, n_lat_tiles, w1, b1, w2, b2, final_w=None):
    b, r, d = h.shape
    t = b * r
    info, dest_tiles, block_expert, n_used, n_blocks = route(logits.reshape(t, LANES))
    x_buf = dispatch(f_packed.reshape(t, d // 2), dest_tiles, n_blocks * MOE_TILE)
    y_buf = expert_ffn(x_buf, block_expert, n_used, w1, b1, w2, b2)
    return combine(h, y_buf, info.reshape(b, r, LANES), dest_tiles, g2, n_lat_tiles, final_w)


def _mod_table(rows, n_batch):
    lat = jnp.stack([r[:n_batch] for r in rows], axis=1)
    ctx = jnp.stack([r[n_batch] for r in rows], axis=0)
    return jnp.stack([jnp.broadcast_to(ctx[None], lat.shape), lat], axis=1)


def kernel(x, c, ctx, c_ctx, mod_w, mod_b, norm1_w, norm2_w, even_w_in, even_w_out, diff_lam, diff_norm_w,
           ret_decay_logit, ret_norm_w, odd_w_qkv, odd_w_out, odd_sinks, router_w, router_b,
           moe_w1, moe_b1, moe_w2, moe_b2, final_norm_w):
    b, s, d = x.shape
    n_ctx = ctx.shape[1]
    n_lat_tiles = s // ROW_TILE
    assert n_ctx == ROW_TILE and s % ROW_TILE == 0 and b + 1 <= 8

    cvec = jnp.zeros((8, d), F32).at[:b].set(c).at[b].set(c_ctx)
    mods = adaln_all(cvec, mod_w, mod_b)
    rope_tabs = rope_tables(s, n_ctx)

    sh1, sc1, g1, sh2, sc2, g2 = jnp.split(mods[0], 6, axis=-1)
    h = jnp.concatenate([x, ctx], axis=1)
    groups0 = ((True, 0.125),) * 4 + ((True, 1.0),) * 4 + ((False, 1.0),) * 4 \
        + ((False, 0.125),) * 2 + ((False, 1.0),) * 10
    p0 = proj_in(h, _mod_table([sh1, sc1], b), norm1_w[0], even_w_in[0].astype(BF16), rope_tabs, groups0,
                 n_lat_tiles)
    lam_init = 0.8 - 0.6 * math.exp(-0.3 * 0)
    d_out = diff_attention(p0, diff_lam[0], diff_norm_w[0], lam_init, n_ctx)
    r_out = retention(p0, ret_decay_logit[0], ret_norm_w[0])
    h, f, logits = post_mixer([d_out, r_out], even_w_out[0].astype(BF16), h, _mod_table([g1, sh2, sc2], b),
                              norm2_w[0], router_w[0], router_b[0], n_lat_tiles)
    h = moe_layer(h, f, logits, _mod_table([g2], b), n_lat_tiles, moe_w1[0], moe_b1[0], moe_w2[0], moe_b2[0])

    sh1, sc1, g1, sh2, sc2, g2 = jnp.split(mods[1], 6, axis=-1)
    groups1 = ((True, 0.125),) * 8 + ((True, 1.0),) + ((False, 1.0),)
    p1 = proj_in(h, _mod_table([sh1, sc1], b), norm1_w[1], odd_w_qkv[0].astype(BF16), rope_tabs, groups1,
                 n_lat_tiles)
    a_out = window_attention(p1, odd_sinks[0], n_ctx)
    h_lat, f, logits = post_mixer([a_out], odd_w_out[0].astype(BF16), h, _mod_table([g1, sh2, sc2], b),
                                  norm2_w[1], router_w[1], router_b[1], n_lat_tiles)
    return moe_layer(h_lat, f, logits, _mod_table([g2], b), n_lat_tiles, moe_w1[1], moe_b1[1], moe_w2[1],
                     moe_b2[1], final_w=final_norm_w)
```

```python
import functools
import math

import jax
import jax.numpy as jnp
from jax import lax
from jax.experimental import pallas as pl
from jax.experimental.pallas import tpu as pltpu

F32 = jnp.float32
BF16 = jnp.bfloat16

GRID_W = 64
EPS = 1e-6
ROPE_BASE = 10000.0
NEG_INF = -1e30
LOG2E = math.log2(math.e)

DA_HEADS = 4
DA_DIM = 64
RET_HEADS = 4
RET_QK = 64
RET_V = 128
SW_Q_HEADS = 16
SW_KV_HEADS = 2
SW_DIM = 64
WINDOW = 128
N_EXPERTS = 32
TOP_K = 4
SWIGLU_ALPHA = 1.702
SWIGLU_LIMIT = 7.0

LANES = 128
ROW_TILE = 256
MOE_TILE = 256
DATTN_TQ = 512
DATTN_TK = 768
FF_CHUNK = 512
VMEM_LIMIT = 48 * 1024 * 1024
FFN_VMEM_LIMIT = 56 * 1024 * 1024

E_DQ, E_DK, E_DV, E_RQ, E_RK, E_RV, E_RG = 0, 4, 8, 12, 14, 16, 20


def _cparams(sem):
    return pltpu.CompilerParams(dimension_semantics=sem, vmem_limit_bytes=VMEM_LIMIT)


def _dot(a, b):
    return jnp.dot(a, b, preferred_element_type=F32)


def _dot_nt(a, b):
    return lax.dot_general(a, b, (((1,), (1,)), ((), ())), preferred_element_type=F32)


def _dot_tn(a, b):
    return lax.dot_general(a, b, (((0,), (0,)), ((), ())), preferred_element_type=F32)


def _sigmoid(x):
    return 1.0 / (1.0 + jnp.exp(-x))


def _pack_pairs(x):
    w = x.shape[1] // 2
    lo = lax.bitcast_convert_type(x[:, :w].astype(BF16).astype(F32), jnp.uint32)
    hi = lax.bitcast_convert_type(x[:, w:].astype(BF16).astype(F32), jnp.uint32)
    return (hi & jnp.uint32(0xFFFF0000)) | (lo >> 16)


def _unpack_pairs(u):
    lo = lax.bitcast_convert_type(u << 16, F32)
    hi = lax.bitcast_convert_type(u & jnp.uint32(0xFFFF0000), F32)
    return jnp.concatenate([lo, hi], axis=1)


def _adaln_kernel(c_ref, w_ref, b_ref, o_ref):
    c = c_ref[...]
    s = c * _sigmoid(c)
    o_ref[...] = _dot(s.astype(BF16), w_ref[...].astype(BF16)) + b_ref[...]


def adaln_all(cvec, mod_w, mod_b):
    depth, d, d6 = mod_w.shape
    tn = 1024
    return pl.pallas_call(
        _adaln_kernel,
        grid=(depth, d6 // tn),
        in_specs=[
            pl.BlockSpec((8, d), lambda l, j: (0, 0)),
            pl.BlockSpec((None, d, tn), lambda l, j: (l, 0, j)),
            pl.BlockSpec((None, 1, tn), lambda l, j: (l, 0, j)),
        ],
        out_specs=pl.BlockSpec((None, 8, tn), lambda l, j: (l, 0, j)),
        out_shape=jax.ShapeDtypeStruct((depth, 8, d6), F32),
        compiler_params=_cparams(("arbitrary", "arbitrary")),
        name="adaln",
    )(cvec, mod_w, mod_b.reshape(depth, 1, d6))


def _norm_mod(x, nw, shift, scale):
    y = x * lax.rsqrt(jnp.mean(x * x, axis=-1, keepdims=True) + EPS)
    return (y * nw) * (1.0 + scale) + shift


def _proj_kernel(h_ref, mod_ref, nw_ref, w_ref, cos_ref, sa_ref, sb_ref, o_ref, *, groups, chunk):
    m = mod_ref[...]
    a = _norm_mod(h_ref[...], nw_ref[...], m[0:1], m[1:2]).astype(BF16)
    cos = cos_ref[...]
    sa = sa_ref[...]
    sb = sb_ref[...]
    gpc = chunk // LANES
    for c in range(len(groups) // gpc):
        acc = _dot(a, w_ref[:, c * chunk:(c + 1) * chunk])
        for gi in range(gpc):
            rope, scale = groups[c * gpc + gi]
            blk = acc[:, gi * LANES:(gi + 1) * LANES]
            if rope:
                blk = blk * cos + pltpu.roll(blk, LANES - 16, 1) * sa + pltpu.roll(blk, 16, 1) * sb
            if scale != 1.0:
                blk = blk * scale
            lo = (c * gpc + gi) * LANES
            o_ref[:, lo:lo + LANES] = blk.astype(BF16)


def proj_in(h, mod, nw, w, rope_tabs, groups, n_lat_tiles):
    b, n, d = h.shape
    dout = w.shape[1]
    tm = ROW_TILE
    chunk = 512 if dout % 512 == 0 else 256
    cos, sa, sb = rope_tabs
    tab_spec = pl.BlockSpec((tm, LANES), lambda bi, i: (i, 0))
    return pl.pallas_call(
        functools.partial(_proj_kernel, groups=groups, chunk=chunk),
        grid=(b, n // tm),
        in_specs=[
            pl.BlockSpec((None, tm, d), lambda bi, i: (bi, i, 0)),
            pl.BlockSpec((None, None, 2, d), lambda bi, i: (bi, jnp.where(i < n_lat_tiles, 1, 0), 0, 0)),
            pl.BlockSpec((1, d), lambda bi, i: (0, 0)),
            pl.BlockSpec((d, dout), lambda bi, i: (0, 0)),
            tab_spec, tab_spec, tab_spec,
        ],
        out_specs=pl.BlockSpec((None, tm, dout), lambda bi, i: (bi, i, 0)),
        out_shape=jax.ShapeDtypeStruct((b, n, dout), BF16),
        compiler_params=_cparams(("arbitrary", "arbitrary")),
        name="proj_in",
    )(h, mod, nw.reshape(1, d), w, cos, sa, sb)


def rope_tables(n_lat, n_ctx):
    f = DA_DIM // 4
    inv = ROPE_BASE ** (-jnp.arange(f, dtype=F32) / f)
    t = jnp.arange(n_lat)
    row = (t // GRID_W).astype(F32)
    col = (t % GRID_W).astype(F32)
    ang_r = row[:, None] * inv[None, :]
    ang_c = col[:, None] * inv[None, :]
    ang = jnp.concatenate([ang_r, ang_r, ang_c, ang_c], axis=-1)
    ang = jnp.concatenate([ang, jnp.zeros((n_ctx, 4 * f), F32)], axis=0)
    cos = jnp.cos(ang)
    sin = jnp.sin(ang)
    first = (jnp.arange(4 * f) % (2 * f)) < f
    sa = jnp.where(first[None, :], -sin, 0.0)
    sb = jnp.where(first[None, :], 0.0, sin)
    rep = LANES // (4 * f)
    return tuple(jnp.tile(x, (1, rep)) for x in (cos, sa, sb))


def _dattn_kernel(q_ref, k_ref, v_ref, lam_ref, nw_ref, o_ref, acc_sc, sa_sc, sb_sc, *, tk, lam_init):
    tq = q_ref.shape[0]
    n_chunks = k_ref.shape[0] // tk
    assert n_chunks % 2 == 1
    q = q_ref[...]
    lane = lax.broadcasted_iota(jnp.int32, q.shape, 1)
    zero = jnp.zeros_like(q)
    qs = (jnp.where(lane < DA_DIM, q, zero), jnp.where(lane >= DA_DIM, q, zero))
    acc_sc[...] = jnp.zeros(acc_sc.shape, F32)

    def scores(c, s_sc):
        k = k_ref[pl.ds(pl.multiple_of(c * tk, tk), tk), :]
        mx = []
        for sub in range(2):
            s = _dot_nt(k, qs[sub])
            s_sc[sub] = s
            mx.append(jnp.max(s, axis=0, keepdims=True))
        return tuple(mx)

    def absorb(c, s_sc, mx, ms, ls):
        v = v_ref[pl.ds(pl.multiple_of(c * tk, tk), tk), :]
        new_m, new_l = [], []
        for sub in range(2):
            m_new = jnp.maximum(ms[sub], mx[sub])
            alpha = jnp.exp2(ms[sub] - m_new)
            p = jnp.exp2(s_sc[sub] - m_new)
            new_l.append(alpha * ls[sub] + jnp.sum(p, axis=0, keepdims=True))
            acc_sc[sub] = alpha * acc_sc[sub] + _dot_tn(v, p.astype(BF16))
            new_m.append(m_new)
        return tuple(new_m), tuple(new_l)

    def body(j, carry):
        mx_a, ms, ls = carry
        mx_b = scores(2 * j + 1, sb_sc)
        ms, ls = absorb(2 * j, sa_sc, mx_a, ms, ls)
        mx_a = scores(2 * j + 2, sa_sc)
        ms, ls = absorb(2 * j + 1, sb_sc, mx_b, ms, ls)
        return mx_a, ms, ls

    m0 = jnp.full((1, tq), -jnp.inf, F32)
    l0 = jnp.zeros((1, tq), F32)
    mx_a, ms, ls = lax.fori_loop(0, n_chunks // 2, body, (scores(0, sa_sc), (m0, m0), (l0, l0)))
    _, ls = absorb(n_chunks - 1, sa_sc, mx_a, ms, ls)

    lp = lam_ref[...]
    lam = (jnp.exp(jnp.sum(lp[0:1] * lp[1:2], axis=-1, keepdims=True))
           - jnp.exp(jnp.sum(lp[2:3] * lp[3:4], axis=-1, keepdims=True))) + lam_init
    o = acc_sc[0] * (1.0 / ls[0]) - lam * (acc_sc[1] * (1.0 / ls[1]))
    y = o * lax.rsqrt(jnp.mean(o * o, axis=0, keepdims=True) + EPS)
    y = (y * nw_ref[...]) * (1.0 - lam_init)
    o_ref[...] = y.T.astype(o_ref.dtype)


def _dattn_call(p0, lam_p, dnorm_w, lam_init, tq, tk, q_tiles, q_tile0, kv_rows, kv_tile0):
    b = p0.shape[0]
    hd = 2 * DA_DIM
    return pl.pallas_call(
        functools.partial(_dattn_kernel, tk=tk, lam_init=lam_init),
        grid=(b, DA_HEADS, q_tiles),
        in_specs=[
            pl.BlockSpec((None, tq, hd), lambda bi, h, i: (bi, i + q_tile0, E_DQ + h)),
            pl.BlockSpec((None, kv_rows, hd), lambda bi, h, i: (bi, kv_tile0, E_DK + h)),
            pl.BlockSpec((None, kv_rows, hd), lambda bi, h, i: (bi, kv_tile0, E_DV + h)),
            pl.BlockSpec((4, DA_DIM), lambda bi, h, i: (0, 0)),
            pl.BlockSpec((hd, 1), lambda bi, h, i: (0, 0)),
        ],
        out_specs=pl.BlockSpec((None, tq, hd), lambda bi, h, i: (bi, i, h)),
        out_shape=jax.ShapeDtypeStruct((b, q_tiles * tq, DA_HEADS * hd), BF16),
        scratch_shapes=[pltpu.VMEM((2, hd, tq), F32), pltpu.VMEM((2, tk, tq), F32), pltpu.VMEM((2, tk, tq), F32)],
        compiler_params=_cparams(("arbitrary", "arbitrary", "arbitrary")),
        name="diff_attn",
    )(p0, p0, p0, lam_p, dnorm_w.reshape(hd, 1))


def diff_attention(p0, lam_p, dnorm_w, lam_init, n_ctx):
    n = p0.shape[1]
    s = n - n_ctx
    tq = min(DATTN_TQ, s)
    tk = DATTN_TK if n % DATTN_TK == 0 else n_ctx
    assert s % tq == 0 and n % tk == 0 and s % n_ctx == 0
    lat = _dattn_call(p0, lam_p, dnorm_w, lam_init, tq, tk, s // tq, 0, n, 0)
    ctx = _dattn_call(p0, lam_p, dnorm_w, lam_init, n_ctx, n_ctx, 1, s // n_ctx, n_ctx, s // n_ctx)
    return jnp.concatenate([lat, ctx], axis=1)


def _ret_kernel(*refs, reverse, finalize):
    if finalize:
        q_ref, k_ref, v_ref, lg_ref, ob_ref, g_ref, nw_ref, o_ref, s_sc = refs
    else:
        q_ref, k_ref, v_ref, lg_ref, o_ref, s_sc = refs
    j = pl.program_id(1)
    c = q_ref.shape[0]

    @pl.when(j == 0)
    def _():
        s_sc[...] = jnp.zeros(s_sc.shape, F32)

    ri = lax.broadcasted_iota(jnp.int32, (c, c), 0)
    ci = lax.broadcasted_iota(jnp.int32, (c, c), 1)
    dist = (ci - ri) if reverse else (ri - ci)
    distf = jnp.maximum(dist, 0).astype(F32)
    pos = lax.broadcasted_iota(jnp.int32, (c, RET_QK), 0).astype(F32)
    q_exp = (c - pos) if reverse else (pos + 1.0)
    k_exp = pos if reverse else (c - 1.0 - pos)

    for h in range(RET_HEADS):
        x = lg_ref[h]
        lg = -jnp.log(1.0 + jnp.exp(-x))
        lgq = lg[:, :RET_QK]
        q = q_ref[:, h * RET_QK:(h + 1) * RET_QK].astype(F32)
        k = k_ref[:, h * RET_QK:(h + 1) * RET_QK].astype(F32)
        v = v_ref[:, h * RET_V:(h + 1) * RET_V]
        decay = jnp.where(dist >= 0, jnp.exp(lg[:, :1] * distf), 0.0)
        scores = _dot_nt(q.astype(BF16), k.astype(BF16)) * decay
        inner = _dot(scores.astype(BF16), v)
        q_dec = q * jnp.exp(lgq * q_exp)
        k_dec = k * jnp.exp(lgq * k_exp)
        state = s_sc[h]
        out = inner + _dot(q_dec.astype(BF16), state.astype(BF16))
        s_sc[h] = jnp.exp(lg * float(c)) * state + _dot_tn(k_dec.astype(BF16), v)
        if finalize:
            r = out + ob_ref[:, h * RET_V:(h + 1) * RET_V]
            y = r * lax.rsqrt(jnp.mean(r * r, axis=-1, keepdims=True) + EPS) * nw_ref[...]
            g = g_ref[:, h * RET_V:(h + 1) * RET_V].astype(F32)
            o_ref[:, h * RET_V:(h + 1) * RET_V] = (y * (g * _sigmoid(g))).astype(o_ref.dtype)
        else:
            o_ref[:, h * RET_V:(h + 1) * RET_V] = out


def retention(p0, decay_logit, rnorm_w):
    b, n, _ = p0.shape
    c = ROW_TILE
    nc = n // c
    logit = jnp.broadcast_to(decay_logit.astype(F32)[:, :, None, None], (2, RET_HEADS, 1, LANES))
    qw, vw = RET_HEADS * RET_QK, RET_HEADS * RET_V

    def specs(cmap):
        return [
            pl.BlockSpec((None, c, qw), lambda bi, j: (bi, cmap(j), E_RQ * LANES // qw)),
            pl.BlockSpec((None, c, qw), lambda bi, j: (bi, cmap(j), E_RK * LANES // qw)),
            pl.BlockSpec((None, c, vw), lambda bi, j: (bi, cmap(j), E_RV * LANES // vw)),
        ]

    lg_spec = lambda d: pl.BlockSpec((None, RET_HEADS, 1, LANES), lambda bi, j: (d, 0, 0, 0))
    scratch = [pltpu.VMEM((RET_HEADS, RET_QK, RET_V), F32)]
    bmap = lambda j: jnp.where(j == 0, nc - 1, nc - 1 - j)
    o_b = pl.pallas_call(
        functools.partial(_ret_kernel, reverse=True, finalize=False),
        grid=(b, nc),
        in_specs=specs(bmap) + [lg_spec(1)],
        out_specs=pl.BlockSpec((None, c, vw), lambda bi, j: (bi, bmap(j), 0)),
        out_shape=jax.ShapeDtypeStruct((b, n, vw), F32),
        scratch_shapes=scratch,
        compiler_params=_cparams(("arbitrary", "arbitrary")),
        name="retention_bwd",
    )(p0, p0, p0, logit)
    fmap = lambda j: jnp.where(j == 0, nc - 1, j - 1)
    return pl.pallas_call(
        functools.partial(_ret_kernel, reverse=False, finalize=True),
        grid=(b, nc),
        in_specs=specs(fmap) + [
            lg_spec(0),
            pl.BlockSpec((None, c, vw), lambda bi, j: (bi, fmap(j), 0)),
            pl.BlockSpec((None, c, vw), lambda bi, j: (bi, fmap(j), E_RG * LANES // vw)),
            pl.BlockSpec((1, RET_V), lambda bi, j: (0, 0)),
        ],
        out_specs=pl.BlockSpec((None, c, vw), lambda bi, j: (bi, fmap(j), 0)),
        out_shape=jax.ShapeDtypeStruct((b, n, vw), BF16),
        scratch_shapes=scratch,
        compiler_params=_cparams(("arbitrary", "arbitrary")),
        name="retention_fwd",
    )(p0, p0, p0, logit, o_b, p0, rnorm_w.reshape(1, RET_V))


def _wattn_kernel(q_ref, kc_ref, vc_ref, kp_ref, vp_ref, k0_ref, v0_ref, kn_ref, vn_ref, sink_ref, o_ref,
                  *, n_tiles, group):
    i = pl.program_id(1)
    tq = q_ref.shape[0]
    kcat = jnp.concatenate([kc_ref[...], kp_ref[...], k0_ref[...], kn_ref[...]], axis=0)
    vcat = jnp.concatenate([vc_ref[...], vp_ref[...], v0_ref[...], vn_ref[...]], axis=0)
    kk = lax.broadcasted_iota(jnp.int32, (4 * tq, tq), 0)
    qq = lax.broadcasted_iota(jnp.int32, (4 * tq, tq), 1)
    seg = kk // tq
    delta = qq - (kk - seg * tq) + (2 - seg) * tq
    far = 4 * tq
    miss = jnp.where(seg == 1, jnp.where(i > 0, 0, far), jnp.where(seg == 3, jnp.where(i < n_tiles - 1, 0, far), 0))
    dist = jnp.where(seg == 0, 0, jnp.abs(delta) + miss)
    bias1 = jnp.where(dist <= WINDOW, 0.0, NEG_INF).astype(F32)
    bias = jnp.concatenate([bias1, bias1], axis=1)
    lane = lax.broadcasted_iota(jnp.int32, (tq, LANES), 1)
    lo_half = lane < SW_DIM
    for hk in range(SW_KV_HEADS):
        kh = kcat[:, hk * SW_DIM:(hk + 1) * SW_DIM]
        kdup = jnp.concatenate([kh, kh], axis=1)
        for pg in range(group // 2):
            pair = hk * (group // 2) + pg
            qp = q_ref[:, pair * LANES:(pair + 1) * LANES]
            zero = jnp.zeros_like(qp)
            qstack = jnp.concatenate([jnp.where(lo_half, qp, zero), jnp.where(lo_half, zero, qp)], axis=0)
            s = _dot_nt(kdup, qstack) + bias
            sink = jnp.concatenate([jnp.broadcast_to(sink_ref[2 * pair][:, :1], (1, tq)),
                                    jnp.broadcast_to(sink_ref[2 * pair + 1][:, :1], (1, tq))], axis=1)
            sink = sink * LOG2E
            m = jnp.maximum(jnp.max(s, axis=0, keepdims=True), sink)
            p = jnp.exp2(s - m)
            den = jnp.sum(p, axis=0, keepdims=True) + jnp.exp2(sink - m)
            ot = _dot_tn(vcat, p.astype(BF16)) * (1.0 / den)
            oth = ot[hk * SW_DIM:(hk + 1) * SW_DIM]
            both = jnp.concatenate([oth[:, :tq], oth[:, tq:]], axis=0)
            o_ref[:, pair * LANES:(pair + 1) * LANES] = both.T.astype(o_ref.dtype)


def window_attention(p1, sinks, n_ctx):
    b, n, _ = p1.shape
    tq = ROW_TILE
    assert n_ctx == tq and WINDOW <= tq
    n_tiles = (n - n_ctx) // tq
    group = SW_Q_HEADS // SW_KV_HEADS
    qw = SW_Q_HEADS * SW_DIM
    kvw = SW_KV_HEADS * SW_DIM
    assert kvw == LANES and group % 2 == 0
    k_col = qw // kvw
    v_col = k_col + 1
    sink_b = jnp.broadcast_to(sinks.astype(F32).reshape(SW_Q_HEADS, 1, 1), (SW_Q_HEADS, 1, LANES))

    def kv(col, tile):
        return pl.BlockSpec((None, tq, kvw), lambda bi, i: (bi, tile(i), col))

    prev = lambda i: jnp.maximum(i - 1, 0)
    cur = lambda i: i
    nxt = lambda i: jnp.minimum(i + 1, n_tiles - 1)
    ctx = lambda i: n_tiles
    return pl.pallas_call(
        functools.partial(_wattn_kernel, n_tiles=n_tiles, group=group),
        grid=(b, n_tiles),
        in_specs=[
            pl.BlockSpec((None, tq, qw), lambda bi, i: (bi, i, 0)),
            kv(k_col, ctx), kv(v_col, ctx), kv(k_col, prev), kv(v_col, prev),
            kv(k_col, cur), kv(v_col, cur), kv(k_col, nxt), kv(v_col, nxt),
            pl.BlockSpec((SW_Q_HEADS, 1, LANES), lambda bi, i: (0, 0, 0)),
        ],
        out_specs=pl.BlockSpec((None, tq, qw), lambda bi, i: (bi, i, 0)),
        out_shape=jax.ShapeDtypeStruct((b, n - n_ctx, qw), BF16),
        compiler_params=_cparams(("arbitrary", "arbitrary")),
        name="window_attn",
    )(p1, p1, p1, p1, p1, p1, p1, p1, p1, sink_b)


def _post_kernel(*refs, n_pieces):
    pieces = refs[:n_pieces]
    w_ref, h_ref, mod_ref, nw_ref, rw_ref, rb_ref, ho_ref, f_ref, lg_ref = refs[n_pieces:]
    m = None
    lo = 0
    for p in pieces:
        wdt = p.shape[1]
        t = _dot(p[...], w_ref[lo:lo + wdt, :])
        m = t if m is None else m + t
        lo += wdt
    md = mod_ref[...]
    hn = h_ref[...] + md[0:1] * m
    ho_ref[...] = hn
    f = _norm_mod(hn, nw_ref[...], md[1:2], md[2:3])
    f_ref[...] = _pack_pairs(f)
    lg_ref[...] = _dot(f.astype(BF16), rw_ref[...]) + rb_ref[...]


def post_mixer(pieces, w_out, h, mod, nw, router_w, router_b, n_lat_tiles):
    b, r, _ = pieces[0].shape
    d = h.shape[2]
    tm = ROW_TILE
    rw = jnp.zeros((d, LANES), BF16).at[:, :N_EXPERTS].set(router_w.astype(BF16))
    rb = jnp.full((1, LANES), NEG_INF, F32).at[0, :N_EXPERTS].set(router_b.astype(F32))
    row = lambda bi, i: (bi, i, 0)
    const2 = lambda bi, i: (0, 0)
    outs = pl.pallas_call(
        functools.partial(_post_kernel, n_pieces=len(pieces)),
        grid=(b, r // tm),
        in_specs=[pl.BlockSpec((None, tm, p.shape[2]), row) for p in pieces] + [
            pl.BlockSpec(w_out.shape, const2),
            pl.BlockSpec((None, tm, d), row),
            pl.BlockSpec((None, None, 3, d), lambda bi, i: (bi, jnp.where(i < n_lat_tiles, 1, 0), 0, 0)),
            pl.BlockSpec((1, d), const2),
            pl.BlockSpec((d, LANES), const2),
            pl.BlockSpec((1, LANES), const2),
        ],
        out_specs=[pl.BlockSpec((None, tm, d), row), pl.BlockSpec((None, tm, d // 2), row),
                   pl.BlockSpec((None, tm, LANES), row)],
        out_shape=[jax.ShapeDtypeStruct((b, r, d), F32), jax.ShapeDtypeStruct((b, r, d // 2), jnp.uint32),
                   jax.ShapeDtypeStruct((b, r, LANES), F32)],
        compiler_params=_cparams(("arbitrary", "arbitrary")),
        name="post_mixer",
    )(*pieces, w_out, h, mod, nw.reshape(1, d), rw, rb)
    return outs


def _router_kernel(lg_ref, info_ref, cnt_ref, base_sc):
    i = pl.program_id(0)
    tm = lg_ref.shape[0]

    @pl.when(i == 0)
    def _():
        base_sc[...] = jnp.zeros(base_sc.shape, F32)

    l = lg_ref[...]
    lane = lax.broadcasted_iota(jnp.int32, l.shape, 1)
    lanef = lane.astype(F32)
    hots, vals, idxs = [], [], []
    for _ in range(TOP_K):
        m = jnp.max(l, axis=1, keepdims=True)
        idx = jnp.min(jnp.where(l == m, lanef, float(LANES)), axis=1, keepdims=True)
        hot = lanef == idx
        hots.append(hot)
        vals.append(m)
        idxs.append(idx)
        l = jnp.where(hot, -jnp.inf, l)
    es = [jnp.exp(v - vals[0]) for v in vals]
    tot = es[0] + es[1] + es[2] + es[3]
    sel = jnp.zeros(l.shape, F32)
    for hot in hots:
        sel = sel + jnp.where(hot, 1.0, 0.0)
    ri = lax.broadcasted_iota(jnp.int32, (tm, tm), 0)
    ci = lax.broadcasted_iota(jnp.int32, (tm, tm), 1)
    earlier = jnp.where(ri > ci, 1.0, 0.0).astype(BF16)
    before = _dot(earlier, sel.astype(BF16)) + base_sc[...]
    info = jnp.zeros(l.shape, F32)
    for k in range(TOP_K):
        pos = jnp.sum(jnp.where(hots[k], before, 0.0), axis=1, keepdims=True)
        info = jnp.where(lane == k, idxs[k], info)
        info = jnp.where(lane == TOP_K + k, pos, info)
        info = jnp.where(lane == 2 * TOP_K + k, es[k] / tot, info)
    info_ref[...] = info
    base_sc[...] = base_sc[...] + jnp.sum(sel, axis=0, keepdims=True)
    cnt_ref[...] = jnp.broadcast_to(base_sc[...], cnt_ref.shape)


def route(logits):
    t = logits.shape[0]
    tm = ROW_TILE
    info, cnt = pl.pallas_call(
        _router_kernel,
        grid=(t // tm,),
        in_specs=[pl.BlockSpec((tm, LANES), lambda i: (i, 0))],
        out_specs=[pl.BlockSpec((tm, LANES), lambda i: (i, 0)), pl.BlockSpec((8, LANES), lambda i: (0, 0))],
        out_shape=[jax.ShapeDtypeStruct((t, LANES), F32), jax.ShapeDtypeStruct((8, LANES), F32)],
        scratch_shapes=[pltpu.VMEM((1, LANES), F32)],
        compiler_params=_cparams(("arbitrary",)),
        name="router",
    )(logits)
    counts = cnt[0, :N_EXPERTS].astype(jnp.int32)
    padded = ((counts + MOE_TILE - 1) // MOE_TILE) * MOE_TILE
    pad_ends = jnp.cumsum(padded)
    pad_starts = pad_ends - padded
    e = info[:, :TOP_K].astype(jnp.int32)
    pos = info[:, TOP_K:2 * TOP_K].astype(jnp.int32)
    dest = jnp.take(pad_starts, e) + pos
    n_blocks = -(-(t * TOP_K) // MOE_TILE) + N_EXPERTS
    block_lo = jnp.arange(n_blocks, dtype=jnp.int32) * MOE_TILE
    block_expert = jnp.minimum(jnp.sum((pad_ends[None, :] <= block_lo[:, None]).astype(jnp.int32), axis=1),
                               N_EXPERTS - 1)
    n_used = (pad_ends[-1] // MOE_TILE).astype(jnp.int32).reshape(1)
    dest_tiles = dest.astype(jnp.int32).reshape(t // tm, 1, tm * TOP_K)
    return info, dest_tiles, block_expert, n_used, n_blocks


def _move_rows(tm, dest_ref, make):
    def copies(t):
        return [make(t, k, dest_ref[0, t * TOP_K + k]) for k in range(TOP_K)]

    def start(t, carry):
        for k, cp in enumerate(copies(t)):
            cp.start(priority=k % 2)
        return carry

    def wait(t, carry):
        for cp in copies(t):
            cp.wait()
        return carry

    lax.fori_loop(0, tm, start, 0)
    lax.fori_loop(0, tm, wait, 0)


def _dispatch_kernel(dest_ref, f_ref, xin_ref, xbuf_ref, sem):
    del xin_ref

    def make(t, k, d):
        return pltpu.make_async_copy(f_ref.at[pl.ds(t, 1), :], xbuf_ref.at[pl.ds(d, 1), :], sem)

    _move_rows(f_ref.shape[0], dest_ref, make)


def dispatch(f_packed, dest_tiles, n_slots):
    t, w = f_packed.shape
    tm = ROW_TILE
    x0 = jnp.zeros((n_slots, w), jnp.uint32)
    return pl.pallas_call(
        _dispatch_kernel,
        grid=(t // tm,),
        in_specs=[
            pl.BlockSpec((None, 1, tm * TOP_K), lambda i: (i, 0, 0), memory_space=pltpu.SMEM),
            pl.BlockSpec((tm, w), lambda i: (i, 0)),
            pl.BlockSpec(memory_space=pl.ANY),
        ],
        out_specs=pl.BlockSpec(memory_space=pl.ANY),
        out_shape=jax.ShapeDtypeStruct((n_slots, w), jnp.uint32),
        scratch_shapes=[pltpu.SemaphoreType.DMA(())],
        input_output_aliases={2: 0},
        compiler_params=_cparams(("arbitrary",)),
        name="dispatch",
    )(dest_tiles, f_packed, x0)


def _ffn_kernel(be_ref, nb_ref, x_ref, w1_ref, b1_ref, w2_ref, b2_ref, o_ref, w1_sc, w2_sc):
    i = pl.program_id(0)
    used = i < nb_ref[0]
    fresh = jnp.logical_or(i == 0, be_ref[i] != be_ref[jnp.maximum(i - 1, 0)])

    @pl.when(jnp.logical_and(used, fresh))
    def _():
        w1_sc[...] = w1_ref[...].astype(BF16)
        w2_sc[...] = w2_ref[...].astype(BF16)

    @pl.when(used)
    def _():
        x = _unpack_pairs(x_ref[...]).astype(BF16)
        dff = w2_sc.shape[0]
        acc = None
        for c in range(dff // FF_CHUNK):
            lo = c * FF_CHUNK
            hg = _dot(x, w1_sc[:, lo:lo + FF_CHUNK]) + b1_ref[:, lo:lo + FF_CHUNK]
            hl = _dot(x, w1_sc[:, dff + lo:dff + lo + FF_CHUNK]) + b1_ref[:, dff + lo:dff + lo + FF_CHUNK]
            glu = jnp.minimum(hg, SWIGLU_LIMIT)
            lin = jnp.clip(hl, -SWIGLU_LIMIT, SWIGLU_LIMIT)
            act = glu * _sigmoid(SWIGLU_ALPHA * glu) * (lin + 1.0)
            t = _dot(act.astype(BF16), w2_sc[lo:lo + FF_CHUNK, :])
            acc = t if acc is None else acc + t
        o_ref[...] = _pack_pairs(acc + b2_ref[...])

    @pl.when(jnp.logical_not(used))
    def _():
        o_ref[...] = jnp.zeros(o_ref.shape, o_ref.dtype)


def expert_ffn(x_buf, block_expert, n_used, layer, w1, b1, w2, b2):
    p, w = x_buf.shape
    nl, ne, d, f2 = w1.shape
    dff = w2.shape[2]
    tm = MOE_TILE
    grid_spec = pltpu.PrefetchScalarGridSpec(
        num_scalar_prefetch=2,
        grid=(p // tm,),
        in_specs=[
            pl.BlockSpec((tm, w), lambda i, be, nb: (i, 0)),
            pl.BlockSpec((None, None, d, f2), lambda i, be, nb: (layer, be[i], 0, 0)),
            pl.BlockSpec((None, None, 1, f2), lambda i, be, nb: (layer, be[i], 0, 0)),
            pl.BlockSpec((None, None, dff, d), lambda i, be, nb: (layer, be[i], 0, 0)),
            pl.BlockSpec((None, None, 1, d), lambda i, be, nb: (layer, be[i], 0, 0)),
        ],
        out_specs=pl.BlockSpec((tm, w), lambda i, be, nb: (i, 0)),
        scratch_shapes=[pltpu.VMEM((d, f2), BF16), pltpu.VMEM((dff, d), BF16)],
    )
    return pl.pallas_call(
        _ffn_kernel,
        grid_spec=grid_spec,
        out_shape=jax.ShapeDtypeStruct((p, w), jnp.uint32),
        compiler_params=pltpu.CompilerParams(dimension_semantics=("arbitrary",), vmem_limit_bytes=FFN_VMEM_LIMIT),
        name="expert_ffn",
    )(block_expert, n_used, x_buf, w1, b1.reshape(nl, ne, 1, f2), w2, b2.reshape(nl, ne, 1, d))


def _combine_kernel(*refs, final):
    if final:
        dest_ref, h_ref, info_ref, g2_ref, nw_ref, ybuf_ref, o_ref, gbuf, sem = refs
    else:
        dest_ref, h_ref, info_ref, g2_ref, ybuf_ref, o_ref, gbuf, sem = refs
    tm = h_ref.shape[0]

    def make(t, k, d):
        return pltpu.make_async_copy(ybuf_ref.at[pl.ds(d, 1), :], gbuf.at[k, pl.ds(t, 1), :], sem)

    _move_rows(tm, dest_ref, make)

    info = info_ref[...]
    acc = None
    for k in range(TOP_K):
        t = info[:, 2 * TOP_K + k:2 * TOP_K + k + 1] * _unpack_pairs(gbuf[k])
        acc = t if acc is None else acc + t
    hn = h_ref[...] + g2_ref[...] * acc
    if final:
        y = hn * lax.rsqrt(jnp.mean(hn * hn, axis=-1, keepdims=True) + EPS)
        hn = y * nw_ref[...]
    o_ref[...] = hn


def combine(h, y_buf, info, dest_tiles, g2, n_lat_tiles, final_w=None):
    b, r, d = h.shape
    tm = ROW_TILE
    nt = r // tm
    final = final_w is not None
    in_specs = [
        pl.BlockSpec((None, 1, tm * TOP_K), lambda bi, i: (bi * nt + i, 0, 0), memory_space=pltpu.SMEM),
        pl.BlockSpec((None, tm, d), lambda bi, i: (bi, i, 0)),
        pl.BlockSpec((None, tm, LANES), lambda bi, i: (bi, i, 0)),
        pl.BlockSpec((None, None, 1, d), lambda bi, i: (bi, jnp.where(i < n_lat_tiles, 1, 0), 0, 0)),
    ]
    args = [dest_tiles, h, info, g2]
    if final:
        in_specs.append(pl.BlockSpec((1, d), lambda bi, i: (0, 0)))
        args.append(final_w.reshape(1, d))
    in_specs.append(pl.BlockSpec(memory_space=pl.ANY))
    args.append(y_buf)
    return pl.pallas_call(
        functools.partial(_combine_kernel, final=final),
        grid=(b, nt),
        in_specs=in_specs,
        out_specs=pl.BlockSpec((None, tm, d), lambda bi, i: (bi, i, 0)),
        out_shape=jax.ShapeDtypeStruct((b, r, d), F32),
        scratch_shapes=[pltpu.VMEM((TOP_K, tm, d // 2), jnp.uint32), pltpu.SemaphoreType.DMA(())],
        compiler_params=_cparams(("arbitrary", "arbitrary")),
        name="combine",
    )(*args)


def moe_layer(h, f_packed, logits, g2, n_lat_tiles, layer, w1, b1, w2, b2, final_w=None):
    b, r, d = h.shape
    t = b * r
    info, dest_tiles, block_expert, n_used, n_blocks = route(logits.reshape(t, LANES))
    x_buf = dispatch(f_packed.reshape(t, d // 2), dest_tiles, n_blocks * MOE_TILE)
    y_buf = expert_ffn(x_buf, block_expert, n_used, layer, w1, b1, w2, b2)
    return combine(h, y_buf, info.reshape(b, r, LANES), dest_tiles, g2, n_lat_tiles, final_w)


def _mod_table(rows, n_batch):
    lat = jnp.stack([r[:n_batch] for r in rows], axis=1)
    ctx = jnp.stack([r[n_batch] for r in rows], axis=0)
    return jnp.stack([jnp.broadcast_to(ctx[None], lat.shape), lat], axis=1)


def kernel(x, c, ctx, c_ctx, mod_w, mod_b, norm1_w, norm2_w, even_w_in, even_w_out, diff_lam, diff_norm_w,
           ret_decay_logit, ret_norm_w, odd_w_qkv, odd_w_out, odd_sinks, router_w, router_b,
           moe_w1, moe_b1, moe_w2, moe_b2, final_norm_w):
    b, s, d = x.shape
    n_ctx = ctx.shape[1]
    n_lat_tiles = s // ROW_TILE
    assert n_ctx == ROW_TILE and s % ROW_TILE == 0 and b + 1 <= 8

    cvec = jnp.zeros((8, d), F32).at[:b].set(c).at[b].set(c_ctx)
    mods = adaln_all(cvec, mod_w, mod_b)
    rope_tabs = rope_tables(s, n_ctx)

    sh1, sc1, g1, sh2, sc2, g2 = jnp.split(mods[0], 6, axis=-1)
    h = jnp.concatenate([x, ctx], axis=1)
    groups0 = ((True, 0.125 * LOG2E),) * 4 + ((True, 1.0),) * 4 + ((False, 1.0),) * 4 \
        + ((False, 0.125),) * 2 + ((False, 1.0),) * 10
    p0 = proj_in(h, _mod_table([sh1, sc1], b), norm1_w[0], even_w_in[0].astype(BF16), rope_tabs, groups0,
                 n_lat_tiles)
    lam_init = 0.8 - 0.6 * math.exp(-0.3 * 0)
    d_out = diff_attention(p0, diff_lam[0], diff_norm_w[0], lam_init, n_ctx)
    r_out = retention(p0, ret_decay_logit[0], ret_norm_w[0])
    h, f, logits = post_mixer([d_out, r_out], even_w_out[0].astype(BF16), h, _mod_table([g1, sh2, sc2], b),
                              norm2_w[0], router_w[0], router_b[0], n_lat_tiles)
    h = moe_layer(h, f, logits, _mod_table([g2], b), n_lat_tiles, 0, moe_w1, moe_b1, moe_w2, moe_b2)

    sh1, sc1, g1, sh2, sc2, g2 = jnp.split(mods[1], 6, axis=-1)
    groups1 = ((True, 0.125 * LOG2E),) * 8 + ((True, 1.0),) + ((False, 1.0),)
    p1 = proj_in(h, _mod_table([sh1, sc1], b), norm1_w[1], odd_w_qkv[0].astype(BF16), rope_tabs, groups1,
                 n_lat_tiles)
    a_out = window_attention(p1, odd_sinks[0], n_ctx)
    h_lat, f, logits = post_mixer([a_out], odd_w_out[0].astype(BF16), h, _mod_table([g1, sh2, sc2], b),
                                  norm2_w[1], router_w[1], router_b[1], n_lat_tiles)
    return moe_layer(h_lat, f, logits, _mod_table([g2], b), n_lat_tiles, 1, moe_w1, moe_b1, moe_w2, moe_b2,
                     final_w=final_norm_w)
```

```python
import functools
import math

import jax
import jax.numpy as jnp
from jax import lax
from jax.experimental import pallas as pl
from jax.experimental.pallas import tpu as pltpu

F32 = jnp.float32
BF16 = jnp.bfloat16

GRID_W = 64
EPS = 1e-6
ROPE_BASE = 10000.0
NEG_INF = -1e30
LOG2E = math.log2(math.e)

DA_HEADS = 4
DA_DIM = 64
RET_HEADS = 4
RET_QK = 64
RET_V = 128
SW_Q_HEADS = 16
SW_KV_HEADS = 2
SW_DIM = 64
WINDOW = 128
N_EXPERTS = 32
TOP_K = 4
SWIGLU_ALPHA = 1.702
SWIGLU_LIMIT = 7.0

LANES = 128
ROW_TILE = 256
MOE_TILE = 256
DATTN_TQ = 512
DATTN_TK = 768
FF_CHUNK = 512
ROUTE_TILE = 512
SLOT_ALIGN = 8
RUN_ROWS = 16
STAGE_ROWS = -(-(ROUTE_TILE * TOP_K + N_EXPERTS * (RUN_ROWS - 1)) // LANES) * LANES
STAGE_COLS = 256
VMEM_LIMIT = 48 * 1024 * 1024
FFN_VMEM_LIMIT = 56 * 1024 * 1024

E_DQ, E_DK, E_DV, E_RQ, E_RK, E_RV, E_RG = 0, 4, 8, 12, 14, 16, 20


def _cparams(sem):
    return pltpu.CompilerParams(dimension_semantics=sem, vmem_limit_bytes=VMEM_LIMIT)


def _dot(a, b):
    return jnp.dot(a, b, preferred_element_type=F32)


def _dot_nt(a, b):
    return lax.dot_general(a, b, (((1,), (1,)), ((), ())), preferred_element_type=F32)


def _dot_tn(a, b):
    return lax.dot_general(a, b, (((0,), (0,)), ((), ())), preferred_element_type=F32)


def _sigmoid(x):
    return 1.0 / (1.0 + jnp.exp(-x))


def _pack_halves(lo, hi):
    lo = lax.bitcast_convert_type(lo.astype(BF16).astype(F32), jnp.uint32)
    hi = lax.bitcast_convert_type(hi.astype(BF16).astype(F32), jnp.uint32)
    return (hi & jnp.uint32(0xFFFF0000)) | (lo >> 16)


def _pack_pairs(x):
    w = x.shape[1] // 2
    return _pack_halves(x[:, :w], x[:, w:])


def _unpack_halves(u):
    return (lax.bitcast_convert_type(u << 16, F32), lax.bitcast_convert_type(u & jnp.uint32(0xFFFF0000), F32))


def _unpack_pairs(u):
    return jnp.concatenate(_unpack_halves(u), axis=1)


def _adaln_kernel(c_ref, w_ref, b_ref, o_ref):
    c = c_ref[...]
    s = c * _sigmoid(c)
    o_ref[...] = _dot(s.astype(BF16), w_ref[...].astype(BF16)) + b_ref[...]


def adaln_all(cvec, mod_w, mod_b):
    depth, d, d6 = mod_w.shape
    tn = 1024
    return pl.pallas_call(
        _adaln_kernel,
        grid=(depth, d6 // tn),
        in_specs=[
            pl.BlockSpec((8, d), lambda l, j: (0, 0)),
            pl.BlockSpec((None, d, tn), lambda l, j: (l, 0, j)),
            pl.BlockSpec((None, 1, tn), lambda l, j: (l, 0, j)),
        ],
        out_specs=pl.BlockSpec((None, 8, tn), lambda l, j: (l, 0, j)),
        out_shape=jax.ShapeDtypeStruct((depth, 8, d6), F32),
        compiler_params=_cparams(("arbitrary", "arbitrary")),
        name="adaln",
    )(cvec, mod_w, mod_b.reshape(depth, 1, d6))


def _norm_mod(x, nw, shift, scale):
    y = x * lax.rsqrt(jnp.mean(x * x, axis=-1, keepdims=True) + EPS)
    return (y * nw) * (1.0 + scale) + shift


def _proj_kernel(h_ref, mod_ref, nw_ref, w_ref, cos_ref, sa_ref, sb_ref, o_ref, *, groups, chunk):
    m = mod_ref[...]
    a = _norm_mod(h_ref[...], nw_ref[...], m[0:1], m[1:2]).astype(BF16)
    cos = cos_ref[...]
    sa = sa_ref[...]
    sb = sb_ref[...]
    gpc = chunk // LANES
    for c in range(len(groups) // gpc):
        acc = _dot(a, w_ref[:, c * chunk:(c + 1) * chunk])
        for gi in range(gpc):
            rope, scale = groups[c * gpc + gi]
            blk = acc[:, gi * LANES:(gi + 1) * LANES]
            if rope:
                blk = blk * cos + pltpu.roll(blk, LANES - 16, 1) * sa + pltpu.roll(blk, 16, 1) * sb
            if scale != 1.0:
                blk = blk * scale
            lo = (c * gpc + gi) * LANES
            o_ref[:, lo:lo + LANES] = blk.astype(BF16)


def proj_in(h, mod, nw, w, rope_tabs, groups, n_lat_tiles):
    b, n, d = h.shape
    dout = w.shape[1]
    tm = ROW_TILE
    chunk = 512 if dout % 512 == 0 else 256
    cos, sa, sb = rope_tabs
    tab_spec = pl.BlockSpec((tm, LANES), lambda bi, i: (i, 0))
    return pl.pallas_call(
        functools.partial(_proj_kernel, groups=groups, chunk=chunk),
        grid=(b, n // tm),
        in_specs=[
            pl.BlockSpec((None, tm, d), lambda bi, i: (bi, i, 0)),
            pl.BlockSpec((None, None, 2, d), lambda bi, i: (bi, jnp.where(i < n_lat_tiles, 1, 0), 0, 0)),
            pl.BlockSpec((1, d), lambda bi, i: (0, 0)),
            pl.BlockSpec((d, dout), lambda bi, i: (0, 0)),
            tab_spec, tab_spec, tab_spec,
        ],
        out_specs=pl.BlockSpec((None, tm, dout), lambda bi, i: (bi, i, 0)),
        out_shape=jax.ShapeDtypeStruct((b, n, dout), BF16),
        compiler_params=_cparams(("arbitrary", "arbitrary")),
        name="proj_in",
    )(h, mod, nw.reshape(1, d), w, cos, sa, sb)


def rope_tables(n_lat, n_ctx):
    f = DA_DIM // 4
    inv = ROPE_BASE ** (-jnp.arange(f, dtype=F32) / f)
    t = jnp.arange(n_lat)
    row = (t // GRID_W).astype(F32)
    col = (t % GRID_W).astype(F32)
    ang_r = row[:, None] * inv[None, :]
    ang_c = col[:, None] * inv[None, :]
    ang = jnp.concatenate([ang_r, ang_r, ang_c, ang_c], axis=-1)
    ang = jnp.concatenate([ang, jnp.zeros((n_ctx, 4 * f), F32)], axis=0)
    cos = jnp.cos(ang)
    sin = jnp.sin(ang)
    first = (jnp.arange(4 * f) % (2 * f)) < f
    sa = jnp.where(first[None, :], -sin, 0.0)
    sb = jnp.where(first[None, :], 0.0, sin)
    rep = LANES // (4 * f)
    return tuple(jnp.tile(x, (1, rep)) for x in (cos, sa, sb))


def _dattn_kernel(q_ref, k_ref, v_ref, lam_ref, nw_ref, o_ref, acc_sc, sa_sc, sb_sc, *, tk, lam_init):
    tq = q_ref.shape[0]
    n_chunks = k_ref.shape[0] // tk
    assert n_chunks % 2 == 1
    q = q_ref[...]
    lane = lax.broadcasted_iota(jnp.int32, q.shape, 1)
    zero = jnp.zeros_like(q)
    qs = (jnp.where(lane < DA_DIM, q, zero), jnp.where(lane >= DA_DIM, q, zero))
    acc_sc[...] = jnp.zeros(acc_sc.shape, F32)

    def scores(c, s_sc):
        k = k_ref[pl.ds(pl.multiple_of(c * tk, tk), tk), :]
        mx = []
        for sub in range(2):
            s = _dot_nt(k, qs[sub])
            s_sc[sub] = s
            mx.append(jnp.max(s, axis=0, keepdims=True))
        return tuple(mx)

    def absorb(c, s_sc, mx, ms, ls):
        v = v_ref[pl.ds(pl.multiple_of(c * tk, tk), tk), :]
        new_m, new_l = [], []
        for sub in range(2):
            m_new = jnp.maximum(ms[sub], mx[sub])
            alpha = jnp.exp2(ms[sub] - m_new)
            p = jnp.exp2(s_sc[sub] - m_new)
            new_l.append(alpha * ls[sub] + jnp.sum(p, axis=0, keepdims=True))
            acc_sc[sub] = alpha * acc_sc[sub] + _dot_tn(v, p.astype(BF16))
            new_m.append(m_new)
        return tuple(new_m), tuple(new_l)

    def body(j, carry):
        mx_a, ms, ls = carry
        mx_b = scores(2 * j + 1, sb_sc)
        ms, ls = absorb(2 * j, sa_sc, mx_a, ms, ls)
        mx_a = scores(2 * j + 2, sa_sc)
        ms, ls = absorb(2 * j + 1, sb_sc, mx_b, ms, ls)
        return mx_a, ms, ls

    m0 = jnp.full((1, tq), -jnp.inf, F32)
    l0 = jnp.zeros((1, tq), F32)
    mx_a, ms, ls = lax.fori_loop(0, n_chunks // 2, body, (scores(0, sa_sc), (m0, m0), (l0, l0)))
    _, ls = absorb(n_chunks - 1, sa_sc, mx_a, ms, ls)

    lp = lam_ref[...]
    lam = (jnp.exp(jnp.sum(lp[0:1] * lp[1:2], axis=-1, keepdims=True))
           - jnp.exp(jnp.sum(lp[2:3] * lp[3:4], axis=-1, keepdims=True))) + lam_init
    o = acc_sc[0] * (1.0 / ls[0]) - lam * (acc_sc[1] * (1.0 / ls[1]))
    y = o * lax.rsqrt(jnp.mean(o * o, axis=0, keepdims=True) + EPS)
    y = (y * nw_ref[...]) * (1.0 - lam_init)
    o_ref[...] = y.T.astype(o_ref.dtype)


def _dattn_call(p0, lam_p, dnorm_w, lam_init, tq, tk, q_tiles, q_tile0, kv_rows, kv_tile0):
    b = p0.shape[0]
    hd = 2 * DA_DIM
    return pl.pallas_call(
        functools.partial(_dattn_kernel, tk=tk, lam_init=lam_init),
        grid=(b, DA_HEADS, q_tiles),
        in_specs=[
            pl.BlockSpec((None, tq, hd), lambda bi, h, i: (bi, i + q_tile0, E_DQ + h)),
            pl.BlockSpec((None, kv_rows, hd), lambda bi, h, i: (bi, kv_tile0, E_DK + h)),
            pl.BlockSpec((None, kv_rows, hd), lambda bi, h, i: (bi, kv_tile0, E_DV + h)),
            pl.BlockSpec((4, DA_DIM), lambda bi, h, i: (0, 0)),
            pl.BlockSpec((hd, 1), lambda bi, h, i: (0, 0)),
        ],
        out_specs=pl.BlockSpec((None, tq, hd), lambda bi, h, i: (bi, i, h)),
        out_shape=jax.ShapeDtypeStruct((b, q_tiles * tq, DA_HEADS * hd), BF16),
        scratch_shapes=[pltpu.VMEM((2, hd, tq), F32), pltpu.VMEM((2, tk, tq), F32), pltpu.VMEM((2, tk, tq), F32)],
        compiler_params=_cparams(("arbitrary", "arbitrary", "arbitrary")),
        name="diff_attn",
    )(p0, p0, p0, lam_p, dnorm_w.reshape(hd, 1))


def diff_attention(p0, lam_p, dnorm_w, lam_init, n_ctx):
    n = p0.shape[1]
    s = n - n_ctx
    tq = min(DATTN_TQ, s)
    tk = DATTN_TK if n % DATTN_TK == 0 else n_ctx
    assert s % tq == 0 and n % tk == 0 and s % n_ctx == 0
    lat = _dattn_call(p0, lam_p, dnorm_w, lam_init, tq, tk, s // tq, 0, n, 0)
    ctx = _dattn_call(p0, lam_p, dnorm_w, lam_init, n_ctx, n_ctx, 1, s // n_ctx, n_ctx, s // n_ctx)
    return jnp.concatenate([lat, ctx], axis=1)


def _ret_kernel(*refs, reverse, finalize):
    if finalize:
        q_ref, k_ref, v_ref, lg_ref, ob_ref, g_ref, nw_ref, o_ref, s_sc = refs
    else:
        q_ref, k_ref, v_ref, lg_ref, o_ref, s_sc = refs
    j = pl.program_id(1)
    c = q_ref.shape[0]

    @pl.when(j == 0)
    def _():
        s_sc[...] = jnp.zeros(s_sc.shape, F32)

    ri = lax.broadcasted_iota(jnp.int32, (c, c), 0)
    ci = lax.broadcasted_iota(jnp.int32, (c, c), 1)
    dist = (ci - ri) if reverse else (ri - ci)
    distf = jnp.maximum(dist, 0).astype(F32)
    pos = lax.broadcasted_iota(jnp.int32, (c, RET_QK), 0).astype(F32)
    q_exp = (c - pos) if reverse else (pos + 1.0)
    k_exp = pos if reverse else (c - 1.0 - pos)

    for h in range(RET_HEADS):
        x = lg_ref[h]
        lg = -jnp.log(1.0 + jnp.exp(-x))
        lgq = lg[:, :RET_QK]
        q = q_ref[:, h * RET_QK:(h + 1) * RET_QK].astype(F32)
        k = k_ref[:, h * RET_QK:(h + 1) * RET_QK].astype(F32)
        v = v_ref[:, h * RET_V:(h + 1) * RET_V]
        decay = jnp.where(dist >= 0, jnp.exp(lg[:, :1] * distf), 0.0)
        scores = _dot_nt(q.astype(BF16), k.astype(BF16)) * decay
        inner = _dot(scores.astype(BF16), v)
        q_dec = q * jnp.exp(lgq * q_exp)
        k_dec = k * jnp.exp(lgq * k_exp)
        state = s_sc[h]
        out = inner + _dot(q_dec.astype(BF16), state.astype(BF16))
        s_sc[h] = jnp.exp(lg * float(c)) * state + _dot_tn(k_dec.astype(BF16), v)
        if finalize:
            r = out + ob_ref[:, h * RET_V:(h + 1) * RET_V]
            y = r * lax.rsqrt(jnp.mean(r * r, axis=-1, keepdims=True) + EPS) * nw_ref[...]
            g = g_ref[:, h * RET_V:(h + 1) * RET_V].astype(F32)
            o_ref[:, h * RET_V:(h + 1) * RET_V] = (y * (g * _sigmoid(g))).astype(o_ref.dtype)
        else:
            o_ref[:, h * RET_V:(h + 1) * RET_V] = out


def retention(p0, decay_logit, rnorm_w):
    b, n, _ = p0.shape
    c = ROW_TILE
    nc = n // c
    logit = jnp.broadcast_to(decay_logit.astype(F32)[:, :, None, None], (2, RET_HEADS, 1, LANES))
    qw, vw = RET_HEADS * RET_QK, RET_HEADS * RET_V

    def specs(cmap):
        return [
            pl.BlockSpec((None, c, qw), lambda bi, j: (bi, cmap(j), E_RQ * LANES // qw)),
            pl.BlockSpec((None, c, qw), lambda bi, j: (bi, cmap(j), E_RK * LANES // qw)),
            pl.BlockSpec((None, c, vw), lambda bi, j: (bi, cmap(j), E_RV * LANES // vw)),
        ]

    lg_spec = lambda d: pl.BlockSpec((None, RET_HEADS, 1, LANES), lambda bi, j: (d, 0, 0, 0))
    scratch = [pltpu.VMEM((RET_HEADS, RET_QK, RET_V), F32)]
    bmap = lambda j: jnp.where(j == 0, nc - 1, nc - 1 - j)
    o_b = pl.pallas_call(
        functools.partial(_ret_kernel, reverse=True, finalize=False),
        grid=(b, nc),
        in_specs=specs(bmap) + [lg_spec(1)],
        out_specs=pl.BlockSpec((None, c, vw), lambda bi, j: (bi, bmap(j), 0)),
        out_shape=jax.ShapeDtypeStruct((b, n, vw), F32),
        scratch_shapes=scratch,
        compiler_params=_cparams(("arbitrary", "arbitrary")),
        name="retention_bwd",
    )(p0, p0, p0, logit)
    fmap = lambda j: jnp.where(j == 0, nc - 1, j - 1)
    return pl.pallas_call(
        functools.partial(_ret_kernel, reverse=False, finalize=True),
        grid=(b, nc),
        in_specs=specs(fmap) + [
            lg_spec(0),
            pl.BlockSpec((None, c, vw), lambda bi, j: (bi, fmap(j), 0)),
            pl.BlockSpec((None, c, vw), lambda bi, j: (bi, fmap(j), E_RG * LANES // vw)),
            pl.BlockSpec((1, RET_V), lambda bi, j: (0, 0)),
        ],
        out_specs=pl.BlockSpec((None, c, vw), lambda bi, j: (bi, fmap(j), 0)),
        out_shape=jax.ShapeDtypeStruct((b, n, vw), BF16),
        scratch_shapes=scratch,
        compiler_params=_cparams(("arbitrary", "arbitrary")),
        name="retention_fwd",
    )(p0, p0, p0, logit, o_b, p0, rnorm_w.reshape(1, RET_V))


def _wattn_kernel(q_ref, kc_ref, vc_ref, kp_ref, vp_ref, k0_ref, v0_ref, kn_ref, vn_ref, sink_ref, o_ref,
                  *, n_tiles, group):
    i = pl.program_id(1)
    tq = q_ref.shape[0]
    kcat = jnp.concatenate([kc_ref[...], kp_ref[...], k0_ref[...], kn_ref[...]], axis=0)
    vcat = jnp.concatenate([vc_ref[...], vp_ref[...], v0_ref[...], vn_ref[...]], axis=0)
    kk = lax.broadcasted_iota(jnp.int32, (4 * tq, tq), 0)
    qq = lax.broadcasted_iota(jnp.int32, (4 * tq, tq), 1)
    seg = kk // tq
    delta = qq - (kk - seg * tq) + (2 - seg) * tq
    far = 4 * tq
    miss = jnp.where(seg == 1, jnp.where(i > 0, 0, far), jnp.where(seg == 3, jnp.where(i < n_tiles - 1, 0, far), 0))
    dist = jnp.where(seg == 0, 0, jnp.abs(delta) + miss)
    bias1 = jnp.where(dist <= WINDOW, 0.0, NEG_INF).astype(F32)
    bias = jnp.concatenate([bias1, bias1], axis=1)
    lane = lax.broadcasted_iota(jnp.int32, (tq, LANES), 1)
    lo_half = lane < SW_DIM
    for hk in range(SW_KV_HEADS):
        kh = kcat[:, hk * SW_DIM:(hk + 1) * SW_DIM]
        kdup = jnp.concatenate([kh, kh], axis=1)
        for pg in range(group // 2):
            pair = hk * (group // 2) + pg
            qp = q_ref[:, pair * LANES:(pair + 1) * LANES]
            zero = jnp.zeros_like(qp)
            qstack = jnp.concatenate([jnp.where(lo_half, qp, zero), jnp.where(lo_half, zero, qp)], axis=0)
            s = _dot_nt(kdup, qstack) + bias
            sink = jnp.concatenate([jnp.broadcast_to(sink_ref[2 * pair][:, :1], (1, tq)),
                                    jnp.broadcast_to(sink_ref[2 * pair + 1][:, :1], (1, tq))], axis=1)
            sink = sink * LOG2E
            m = jnp.maximum(jnp.max(s, axis=0, keepdims=True), sink)
            p = jnp.exp2(s - m)
            den = jnp.sum(p, axis=0, keepdims=True) + jnp.exp2(sink - m)
            ot = _dot_tn(vcat, p.astype(BF16)) * (1.0 / den)
            oth = ot[hk * SW_DIM:(hk + 1) * SW_DIM]
            both = jnp.concatenate([oth[:, :tq], oth[:, tq:]], axis=0)
            o_ref[:, pair * LANES:(pair + 1) * LANES] = both.T.astype(o_ref.dtype)


def window_attention(p1, sinks, n_ctx):
    b, n, _ = p1.shape
    tq = ROW_TILE
    assert n_ctx == tq and WINDOW <= tq
    n_tiles = (n - n_ctx) // tq
    group = SW_Q_HEADS // SW_KV_HEADS
    qw = SW_Q_HEADS * SW_DIM
    kvw = SW_KV_HEADS * SW_DIM
    assert kvw == LANES and group % 2 == 0
    k_col = qw // kvw
    v_col = k_col + 1
    sink_b = jnp.broadcast_to(sinks.astype(F32).reshape(SW_Q_HEADS, 1, 1), (SW_Q_HEADS, 1, LANES))

    def kv(col, tile):
        return pl.BlockSpec((None, tq, kvw), lambda bi, i: (bi, tile(i), col))

    prev = lambda i: jnp.maximum(i - 1, 0)
    cur = lambda i: i
    nxt = lambda i: jnp.minimum(i + 1, n_tiles - 1)
    ctx = lambda i: n_tiles
    return pl.pallas_call(
        functools.partial(_wattn_kernel, n_tiles=n_tiles, group=group),
        grid=(b, n_tiles),
        in_specs=[
            pl.BlockSpec((None, tq, qw), lambda bi, i: (bi, i, 0)),
            kv(k_col, ctx), kv(v_col, ctx), kv(k_col, prev), kv(v_col, prev),
            kv(k_col, cur), kv(v_col, cur), kv(k_col, nxt), kv(v_col, nxt),
            pl.BlockSpec((SW_Q_HEADS, 1, LANES), lambda bi, i: (0, 0, 0)),
        ],
        out_specs=pl.BlockSpec((None, tq, qw), lambda bi, i: (bi, i, 0)),
        out_shape=jax.ShapeDtypeStruct((b, n - n_ctx, qw), BF16),
        compiler_params=_cparams(("arbitrary", "arbitrary")),
        name="window_attn",
    )(p1, p1, p1, p1, p1, p1, p1, p1, p1, sink_b)


def _post_kernel(*refs, n_pieces):
    pieces = refs[:n_pieces]
    w_ref, h_ref, mod_ref, nw_ref, rw_ref, rb_ref, ho_ref, f_ref, lg_ref = refs[n_pieces:]
    m = None
    lo = 0
    for p in pieces:
        wdt = p.shape[1]
        t = _dot(p[...], w_ref[lo:lo + wdt, :])
        m = t if m is None else m + t
        lo += wdt
    md = mod_ref[...]
    hn = h_ref[...] + md[0:1] * m
    ho_ref[...] = hn
    f = _norm_mod(hn, nw_ref[...], md[1:2], md[2:3]).astype(BF16)
    f_ref[...] = f
    lg_ref[...] = _dot(f, rw_ref[...]) + rb_ref[...]


def post_mixer(pieces, w_out, h, mod, nw, router_w, router_b, n_lat_tiles):
    b, r, _ = pieces[0].shape
    d = h.shape[2]
    tm = ROW_TILE
    rw = jnp.zeros((d, LANES), BF16).at[:, :N_EXPERTS].set(router_w.astype(BF16))
    rb = jnp.full((1, LANES), NEG_INF, F32).at[0, :N_EXPERTS].set(router_b.astype(F32))
    row = lambda bi, i: (bi, i, 0)
    const2 = lambda bi, i: (0, 0)
    outs = pl.pallas_call(
        functools.partial(_post_kernel, n_pieces=len(pieces)),
        grid=(b, r // tm),
        in_specs=[pl.BlockSpec((None, tm, p.shape[2]), row) for p in pieces] + [
            pl.BlockSpec(w_out.shape, const2),
            pl.BlockSpec((None, tm, d), row),
            pl.BlockSpec((None, None, 3, d), lambda bi, i: (bi, jnp.where(i < n_lat_tiles, 1, 0), 0, 0)),
            pl.BlockSpec((1, d), const2),
            pl.BlockSpec((d, LANES), const2),
            pl.BlockSpec((1, LANES), const2),
        ],
        out_specs=[pl.BlockSpec((None, tm, d), row), pl.BlockSpec((None, tm, d), row),
                   pl.BlockSpec((None, tm, LANES), row)],
        out_shape=[jax.ShapeDtypeStruct((b, r, d), F32), jax.ShapeDtypeStruct((b, r, d), BF16),
                   jax.ShapeDtypeStruct((b, r, LANES), F32)],
        compiler_params=_cparams(("arbitrary", "arbitrary")),
        name="post_mixer",
    )(*pieces, w_out, h, mod, nw.reshape(1, d), rw, rb)
    return outs


def _router_kernel(lg_ref, info_ref, tile_ref, cnt_ref, base_sc):
    i = pl.program_id(0)
    tm = lg_ref.shape[0]

    @pl.when(i == 0)
    def _():
        base_sc[...] = jnp.zeros(base_sc.shape, F32)

    l = lg_ref[...]
    lane = lax.broadcasted_iota(jnp.int32, l.shape, 1)
    lanef = lane.astype(F32)
    hots, vals, idxs = [], [], []
    for _ in range(TOP_K):
        m = jnp.max(l, axis=1, keepdims=True)
        idx = jnp.min(jnp.where(l == m, lanef, float(LANES)), axis=1, keepdims=True)
        hot = lanef == idx
        hots.append(hot)
        vals.append(m)
        idxs.append(idx)
        l = jnp.where(hot, -jnp.inf, l)
    es = [jnp.exp(v - vals[0]) for v in vals]
    tot = es[0] + es[1] + es[2] + es[3]
    sel = jnp.zeros(l.shape, F32)
    for hot in hots:
        sel = sel + jnp.where(hot, 1.0, 0.0)
    ri = lax.broadcasted_iota(jnp.int32, (tm, tm), 0)
    ci = lax.broadcasted_iota(jnp.int32, (tm, tm), 1)
    earlier = jnp.where(ri > ci, 1.0, 0.0).astype(BF16)
    before = _dot(earlier, sel.astype(BF16))
    cnt = jnp.sum(sel, axis=0, keepdims=True)
    run = jnp.floor((cnt + (RUN_ROWS - 1.0)) * (1.0 / RUN_ROWS)) * RUN_ROWS
    li = lax.broadcasted_iota(jnp.int32, (LANES, LANES), 0)
    lj = lax.broadcasted_iota(jnp.int32, (LANES, LANES), 1)
    lower = jnp.where(li < lj, 1.0, 0.0).astype(BF16)
    run_off = _dot(jnp.broadcast_to(run, (8, LANES)).astype(BF16), lower)[0:1]
    where_ = before + run_off
    info = jnp.zeros(l.shape, F32)
    for k in range(TOP_K):
        row = jnp.sum(jnp.where(hots[k], where_, 0.0), axis=1, keepdims=True)
        info = jnp.where(lane == k, idxs[k], info)
        info = jnp.where(lane == TOP_K + k, row, info)
        info = jnp.where(lane == 2 * TOP_K + k, es[k] / tot, info)
    info_ref[...] = info
    sub = lax.broadcasted_iota(jnp.int32, tile_ref.shape, 0)
    tile_ref[...] = jnp.where(sub == 0, cnt, jnp.where(sub == 1, base_sc[...], jnp.where(sub == 2, run_off, 0.0)))
    base_sc[...] = base_sc[...] + jnp.floor((cnt + (SLOT_ALIGN - 1.0)) * (1.0 / SLOT_ALIGN)) * SLOT_ALIGN
    cnt_ref[...] = jnp.broadcast_to(base_sc[...], cnt_ref.shape)


def route(logits):
    t = logits.shape[0]
    tm = ROUTE_TILE
    nt = t // tm
    info, tile, cnt = pl.pallas_call(
        _router_kernel,
        grid=(nt,),
        in_specs=[pl.BlockSpec((tm, LANES), lambda i: (i, 0))],
        out_specs=[pl.BlockSpec((tm, LANES), lambda i: (i, 0)), pl.BlockSpec((None, 8, LANES), lambda i: (i, 0, 0)),
                   pl.BlockSpec((8, LANES), lambda i: (0, 0))],
        out_shape=[jax.ShapeDtypeStruct((t, LANES), F32), jax.ShapeDtypeStruct((nt, 8, LANES), F32),
                   jax.ShapeDtypeStruct((8, LANES), F32)],
        scratch_shapes=[pltpu.VMEM((1, LANES), F32)],
        compiler_params=_cparams(("arbitrary",)),
        name="router",
    )(logits)
    counts = cnt[0, :N_EXPERTS].astype(jnp.int32)
    padded = ((counts + RUN_ROWS + MOE_TILE - 1) // MOE_TILE) * MOE_TILE
    pad_ends = jnp.cumsum(padded)
    pad_starts = pad_ends - padded
    worst = t * TOP_K + nt * N_EXPERTS * (SLOT_ALIGN - 1) + N_EXPERTS * (RUN_ROWS + MOE_TILE - 1)
    n_blocks = -(-worst // MOE_TILE)
    block_lo = jnp.arange(n_blocks, dtype=jnp.int32) * MOE_TILE
    block_expert = jnp.minimum(jnp.sum((pad_ends[None, :] <= block_lo[:, None]).astype(jnp.int32), axis=1),
                               N_EXPERTS - 1)
    n_used = (pad_ends[-1] // MOE_TILE).astype(jnp.int32).reshape(1)
    tcnt = tile[:, 0, :N_EXPERTS].astype(jnp.int32)
    slot0 = pad_starts[None, :] + tile[:, 1, :N_EXPERTS].astype(jnp.int32)
    runs = (tcnt + RUN_ROWS - 1) // RUN_ROWS
    roff = tile[:, 2, :N_EXPERTS].astype(jnp.int32)
    table = jnp.concatenate([slot0, runs, roff, jnp.zeros_like(roff)], axis=1).reshape(nt, 1, 4 * N_EXPERTS)
    return info, table, block_expert, n_used, n_blocks


def _run_copies(table_ref, make):
    def for_each_run(fn):
        def per_expert(e, carry):
            slot0 = table_ref[0, e]
            row0 = table_ref[0, 2 * N_EXPERTS + e]

            def one(j, c):
                fn(make(pl.multiple_of(slot0 + j * RUN_ROWS, SLOT_ALIGN),
                        pl.multiple_of(row0 + j * RUN_ROWS, RUN_ROWS)))
                return c

            return lax.fori_loop(0, table_ref[0, N_EXPERTS + e], one, carry)

        lax.fori_loop(0, N_EXPERTS, per_expert, 0)

    for_each_run(lambda cp: cp.start())
    for_each_run(lambda cp: cp.wait())


def _staging_onehots(info):
    tm = info.shape[0]
    col = lax.broadcasted_iota(jnp.int32, (tm, STAGE_ROWS), 1).astype(F32)
    return [col == info[:, TOP_K + k:TOP_K + k + 1] for k in range(TOP_K)]


def _dispatch_kernel(table_ref, info_ref, f_ref, xin_ref, xbuf_ref, stage, sem):
    del xin_ref
    hots = _staging_onehots(info_ref[...])
    sel = jnp.logical_or(jnp.logical_or(hots[0], hots[1]), jnp.logical_or(hots[2], hots[3]))
    place = jnp.where(sel, 1.0, 0.0).astype(BF16)
    half = f_ref.shape[1] // 2
    for c in range(half // STAGE_COLS):
        lo = c * STAGE_COLS
        stage[:, lo:lo + STAGE_COLS] = _pack_halves(_dot_tn(place, f_ref[:, lo:lo + STAGE_COLS]),
                                                    _dot_tn(place, f_ref[:, half + lo:half + lo + STAGE_COLS]))

    def make(slot, row):
        return pltpu.make_async_copy(stage.at[pl.ds(row, RUN_ROWS), :], xbuf_ref.at[pl.ds(slot, RUN_ROWS), :], sem)

    _run_copies(table_ref, make)


def dispatch(f, info, table, n_slots):
    t, d = f.shape
    tm = ROUTE_TILE
    x0 = jnp.zeros((n_slots, d // 2), jnp.uint32)
    return pl.pallas_call(
        _dispatch_kernel,
        grid=(t // tm,),
        in_specs=[
            pl.BlockSpec((None, 1, 4 * N_EXPERTS), lambda i: (i, 0, 0), memory_space=pltpu.SMEM),
            pl.BlockSpec((tm, LANES), lambda i: (i, 0)),
            pl.BlockSpec((tm, d), lambda i: (i, 0)),
            pl.BlockSpec(memory_space=pl.ANY),
        ],
        out_specs=pl.BlockSpec(memory_space=pl.ANY),
        out_shape=jax.ShapeDtypeStruct((n_slots, d // 2), jnp.uint32),
        scratch_shapes=[pltpu.VMEM((STAGE_ROWS, d // 2), jnp.uint32), pltpu.SemaphoreType.DMA(())],
        input_output_aliases={3: 0},
        compiler_params=_cparams(("arbitrary",)),
        name="dispatch",
    )(table, info, f, x0)


def _ffn_kernel(be_ref, nb_ref, x_ref, w1_ref, b1_ref, w2_ref, b2_ref, o_ref, w1_sc, w2_sc):
    i = pl.program_id(0)
    used = i < nb_ref[0]
    fresh = jnp.logical_or(i == 0, be_ref[i] != be_ref[jnp.maximum(i - 1, 0)])

    @pl.when(jnp.logical_and(used, fresh))
    def _():
        w1_sc[...] = w1_ref[...].astype(BF16)
        w2_sc[...] = w2_ref[...].astype(BF16)

    @pl.when(used)
    def _():
        x = _unpack_pairs(x_ref[...]).astype(BF16)
        dff = w2_sc.shape[0]
        acc = None
        for c in range(dff // FF_CHUNK):
            lo = c * FF_CHUNK
            hg = _dot(x, w1_sc[:, lo:lo + FF_CHUNK]) + b1_ref[:, lo:lo + FF_CHUNK]
            hl = _dot(x, w1_sc[:, dff + lo:dff + lo + FF_CHUNK]) + b1_ref[:, dff + lo:dff + lo + FF_CHUNK]
            glu = jnp.minimum(hg, SWIGLU_LIMIT)
            lin = jnp.clip(hl, -SWIGLU_LIMIT, SWIGLU_LIMIT)
            act = glu * _sigmoid(SWIGLU_ALPHA * glu) * (lin + 1.0)
            t = _dot(act.astype(BF16), w2_sc[lo:lo + FF_CHUNK, :])
            acc = t if acc is None else acc + t
        o_ref[...] = _pack_pairs(acc + b2_ref[...])

    @pl.when(jnp.logical_not(used))
    def _():
        o_ref[...] = jnp.zeros(o_ref.shape, o_ref.dtype)


def expert_ffn(x_buf, block_expert, n_used, layer, w1, b1, w2, b2):
    p, w = x_buf.shape
    nl, ne, d, f2 = w1.shape
    dff = w2.shape[2]
    tm = MOE_TILE
    grid_spec = pltpu.PrefetchScalarGridSpec(
        num_scalar_prefetch=2,
        grid=(p // tm,),
        in_specs=[
            pl.BlockSpec((tm, w), lambda i, be, nb: (i, 0)),
            pl.BlockSpec((None, None, d, f2), lambda i, be, nb: (layer, be[i], 0, 0)),
            pl.BlockSpec((None, None, 1, f2), lambda i, be, nb: (layer, be[i], 0, 0)),
            pl.BlockSpec((None, None, dff, d), lambda i, be, nb: (layer, be[i], 0, 0)),
            pl.BlockSpec((None, None, 1, d), lambda i, be, nb: (layer, be[i], 0, 0)),
        ],
        out_specs=pl.BlockSpec((tm, w), lambda i, be, nb: (i, 0)),
        scratch_shapes=[pltpu.VMEM((d, f2), BF16), pltpu.VMEM((dff, d), BF16)],
    )
    return pl.pallas_call(
        _ffn_kernel,
        grid_spec=grid_spec,
        out_shape=jax.ShapeDtypeStruct((p, w), jnp.uint32),
        compiler_params=pltpu.CompilerParams(dimension_semantics=("arbitrary",), vmem_limit_bytes=FFN_VMEM_LIMIT),
        name="expert_ffn",
    )(block_expert, n_used, x_buf, w1, b1.reshape(nl, ne, 1, f2), w2, b2.reshape(nl, ne, 1, d))


def _combine_kernel(*refs, final):
    if final:
        table_ref, h_ref, info_ref, g2a_ref, g2b_ref, nw_ref, ybuf_ref, o_ref, stage, sem = refs
    else:
        table_ref, h_ref, info_ref, g2a_ref, g2b_ref, ybuf_ref, o_ref, stage, sem = refs

    @pl.when(pl.program_id(0) == 0)
    def _():
        stage[...] = jnp.zeros(stage.shape, stage.dtype)

    def make(slot, row):
        return pltpu.make_async_copy(ybuf_ref.at[pl.ds(slot, RUN_ROWS), :], stage.at[pl.ds(row, RUN_ROWS), :], sem)

    _run_copies(table_ref, make)

    info = info_ref[...]
    hots = _staging_onehots(info)
    wgt = jnp.zeros(hots[0].shape, F32)
    for k in range(TOP_K):
        wgt = jnp.where(hots[k], info[:, 2 * TOP_K + k:2 * TOP_K + k + 1], wgt)
    w_hi = wgt.astype(BF16)
    w_lo = (wgt - w_hi.astype(F32)).astype(BF16)
    half = h_ref.shape[1] // 2
    lows, highs = [], []
    for c in range(half // STAGE_COLS):
        y_lo, y_hi = _unpack_halves(stage[:, c * STAGE_COLS:(c + 1) * STAGE_COLS])
        y_lo = y_lo.astype(BF16)
        y_hi = y_hi.astype(BF16)
        lows.append(_dot(w_hi, y_lo) + _dot(w_lo, y_lo))
        highs.append(_dot(w_hi, y_hi) + _dot(w_lo, y_hi))
    acc = jnp.concatenate(lows + highs, axis=1)
    sub = h_ref.shape[0] // 2
    g2 = jnp.concatenate([jnp.broadcast_to(g2a_ref[...], (sub, acc.shape[1])),
                          jnp.broadcast_to(g2b_ref[...], (sub, acc.shape[1]))], axis=0)
    hn = h_ref[...] + g2 * acc
    if final:
        y = hn * lax.rsqrt(jnp.mean(hn * hn, axis=-1, keepdims=True) + EPS)
        hn = y * nw_ref[...]
    o_ref[...] = hn


def combine(h, y_buf, info, table, g2, n_lat_tiles, final_w=None):
    b, r, d = h.shape
    tm = ROUTE_TILE
    rows_per_batch = r // ROW_TILE

    def g2_spec(j):
        def index(i):
            s = 2 * i + j
            return (s // rows_per_batch, jnp.where(s % rows_per_batch < n_lat_tiles, 1, 0), 0, 0)
        return pl.BlockSpec((None, None, 1, d), index)

    final = final_w is not None
    in_specs = [
        pl.BlockSpec((None, 1, 4 * N_EXPERTS), lambda i: (i, 0, 0), memory_space=pltpu.SMEM),
        pl.BlockSpec((tm, d), lambda i: (i, 0)),
        pl.BlockSpec((tm, LANES), lambda i: (i, 0)),
        g2_spec(0), g2_spec(1),
    ]
    args = [table, h.reshape(b * r, d), info, g2, g2]
    if final:
        in_specs.append(pl.BlockSpec((1, d), lambda i: (0, 0)))
        args.append(final_w.reshape(1, d))
    in_specs.append(pl.BlockSpec(memory_space=pl.ANY))
    args.append(y_buf)
    out = pl.pallas_call(
        functools.partial(_combine_kernel, final=final),
        grid=(b * r // tm,),
        in_specs=in_specs,
        out_specs=pl.BlockSpec((tm, d), lambda i: (i, 0)),
        out_shape=jax.ShapeDtypeStruct((b * r, d), F32),
        scratch_shapes=[pltpu.VMEM((STAGE_ROWS, d // 2), jnp.uint32), pltpu.SemaphoreType.DMA(())],
        compiler_params=_cparams(("arbitrary",)),
        name="combine",
    )(*args)
    return out.reshape(b, r, d)


def moe_layer(h, f, logits, g2, n_lat_tiles, layer, w1, b1, w2, b2, final_w=None):
    b, r, d = h.shape
    t = b * r
    info, table, block_expert, n_used, n_blocks = route(logits.reshape(t, LANES))
    x_buf = dispatch(f.reshape(t, d), info, table, n_blocks * MOE_TILE)
    y_buf = expert_ffn(x_buf, block_expert, n_used, layer, w1, b1, w2, b2)
    return combine(h, y_buf, info, table, g2, n_lat_tiles, final_w)


def _mod_table(rows, n_batch):
    lat = jnp.stack([r[:n_batch] for r in rows], axis=1)
    ctx = jnp.stack([r[n_batch] for r in rows], axis=0)
    return jnp.stack([jnp.broadcast_to(ctx[None], lat.shape), lat], axis=1)


def kernel(x, c, ctx, c_ctx, mod_w, mod_b, norm1_w, norm2_w, even_w_in, even_w_out, diff_lam, diff_norm_w,
           ret_decay_logit, ret_norm_w, odd_w_qkv, odd_w_out, odd_sinks, router_w, router_b,
           moe_w1, moe_b1, moe_w2, moe_b2, final_norm_w):
    b, s, d = x.shape
    n_ctx = ctx.shape[1]
    n_lat_tiles = s // ROW_TILE
    assert n_ctx == ROW_TILE and s % ROW_TILE == 0 and b + 1 <= 8

    cvec = jnp.zeros((8, d), F32).at[:b].set(c).at[b].set(c_ctx)
    mods = adaln_all(cvec, mod_w, mod_b)
    rope_tabs = rope_tables(s, n_ctx)

    sh1, sc1, g1, sh2, sc2, g2 = jnp.split(mods[0], 6, axis=-1)
    h = jnp.concatenate([x, ctx], axis=1)
    groups0 = ((True, 0.125 * LOG2E),) * 4 + ((True, 1.0),) * 4 + ((False, 1.0),) * 4 \
        + ((False, 0.125),) * 2 + ((False, 1.0),) * 10
    p0 = proj_in(h, _mod_table([sh1, sc1], b), norm1_w[0], even_w_in[0].astype(BF16), rope_tabs, groups0,
                 n_lat_tiles)
    lam_init = 0.8 - 0.6 * math.exp(-0.3 * 0)
    d_out = diff_attention(p0, diff_lam[0], diff_norm_w[0], lam_init, n_ctx)
    r_out = retention(p0, ret_decay_logit[0], ret_norm_w[0])
    h, f, logits = post_mixer([d_out, r_out], even_w_out[0].astype(BF16), h, _mod_table([g1, sh2, sc2], b),
                              norm2_w[0], router_w[0], router_b[0], n_lat_tiles)
    h = moe_layer(h, f, logits, _mod_table([g2], b), n_lat_tiles, 0, moe_w1, moe_b1, moe_w2, moe_b2)

    sh1, sc1, g1, sh2, sc2, g2 = jnp.split(mods[1], 6, axis=-1)
    groups1 = ((True, 0.125 * LOG2E),) * 8 + ((True, 1.0),) + ((False, 1.0),)
    p1 = proj_in(h, _mod_table([sh1, sc1], b), norm1_w[1], odd_w_qkv[0].astype(BF16), rope_tabs, groups1,
                 n_lat_tiles)
    a_out = window_attention(p1, odd_sinks[0], n_ctx)
    h_lat, f, logits = post_mixer([a_out], odd_w_out[0].astype(BF16), h, _mod_table([g1, sh2, sc2], b),
                                  norm2_w[1], router_w[1], router_b[1], n_lat_tiles)
    return moe_layer(h_lat, f, logits, _mod_table([g2], b), n_lat_tiles, 1, moe_w1, moe_b1, moe_w2, moe_b2,
                     final_w=final_norm_w)
```

```python
import functools
import math

import jax
import jax.numpy as jnp
from jax import lax
from jax.experimental import pallas as pl
from jax.experimental.pallas import tpu as pltpu

F32 = jnp.float32
BF16 = jnp.bfloat16

GRID_W = 64
EPS = 1e-6
ROPE_BASE = 10000.0
NEG_INF = -1e30
LOG2E = math.log2(math.e)

DA_HEADS = 4
DA_DIM = 64
RET_HEADS = 4
RET_QK = 64
RET_V = 128
SW_Q_HEADS = 16
SW_KV_HEADS = 2
SW_DIM = 64
WINDOW = 128
N_EXPERTS = 32
TOP_K = 4
SWIGLU_ALPHA = 1.702
SWIGLU_LIMIT = 7.0

LANES = 128
ROW_TILE = 256
MOE_TILE = 256
DATTN_TQ = 1024
DATTN_TK = 768
FF_CHUNK = 512
ROUTE_TILE = 512
SLOT_ALIGN = 8
RUN_ROWS = 16
STAGE_ROWS = -(-(ROUTE_TILE * TOP_K + N_EXPERTS * (RUN_ROWS - 1)) // LANES) * LANES
STAGE_COLS = 256
VMEM_LIMIT = 48 * 1024 * 1024
FFN_VMEM_LIMIT = 56 * 1024 * 1024

E_DQ, E_DK, E_DV, E_RQ, E_RK, E_RV, E_RG = 0, 4, 8, 12, 14, 16, 20


def _cparams(sem):
    return pltpu.CompilerParams(dimension_semantics=sem, vmem_limit_bytes=VMEM_LIMIT)


def _dot(a, b):
    return jnp.dot(a, b, preferred_element_type=F32)


def _dot_nt(a, b):
    return lax.dot_general(a, b, (((1,), (1,)), ((), ())), preferred_element_type=F32)


def _dot_tn(a, b):
    return lax.dot_general(a, b, (((0,), (0,)), ((), ())), preferred_element_type=F32)


def _sigmoid(x):
    return 1.0 / (1.0 + jnp.exp(-x))


def _pack_halves(lo, hi):
    lo = lax.bitcast_convert_type(lo.astype(BF16).astype(F32), jnp.uint32)
    hi = lax.bitcast_convert_type(hi.astype(BF16).astype(F32), jnp.uint32)
    return (hi & jnp.uint32(0xFFFF0000)) | (lo >> 16)


def _pack_pairs(x):
    w = x.shape[1] // 2
    return _pack_halves(x[:, :w], x[:, w:])


def _unpack_halves(u):
    return (lax.bitcast_convert_type(u << 16, F32), lax.bitcast_convert_type(u & jnp.uint32(0xFFFF0000), F32))


def _unpack_pairs(u):
    return jnp.concatenate(_unpack_halves(u), axis=1)


def _adaln_kernel(c_ref, w_ref, b_ref, o_ref):
    c = c_ref[...]
    s = c * _sigmoid(c)
    o_ref[...] = _dot(s.astype(BF16), w_ref[...].astype(BF16)) + b_ref[...]


def adaln_all(cvec, mod_w, mod_b):
    depth, d, d6 = mod_w.shape
    tn = 1024
    return pl.pallas_call(
        _adaln_kernel,
        grid=(depth, d6 // tn),
        in_specs=[
            pl.BlockSpec((8, d), lambda l, j: (0, 0)),
            pl.BlockSpec((None, d, tn), lambda l, j: (l, 0, j)),
            pl.BlockSpec((None, 1, tn), lambda l, j: (l, 0, j)),
        ],
        out_specs=pl.BlockSpec((None, 8, tn), lambda l, j: (l, 0, j)),
        out_shape=jax.ShapeDtypeStruct((depth, 8, d6), F32),
        compiler_params=_cparams(("arbitrary", "arbitrary")),
        name="adaln",
    )(cvec, mod_w, mod_b.reshape(depth, 1, d6))


def _norm_mod(x, nw, shift, scale):
    y = x * lax.rsqrt(jnp.mean(x * x, axis=-1, keepdims=True) + EPS)
    return (y * nw) * (1.0 + scale) + shift


def _proj_kernel(h_ref, mod_ref, nw_ref, w_ref, cos_ref, sa_ref, sb_ref, o_ref, *, groups, chunk):
    m = mod_ref[...]
    a = _norm_mod(h_ref[...], nw_ref[...], m[0:1], m[1:2]).astype(BF16)
    cos = cos_ref[...]
    sa = sa_ref[...]
    sb = sb_ref[...]
    gpc = chunk // LANES
    for c in range(len(groups) // gpc):
        acc = _dot(a, w_ref[:, c * chunk:(c + 1) * chunk])
        for gi in range(gpc):
            rope, scale = groups[c * gpc + gi]
            blk = acc[:, gi * LANES:(gi + 1) * LANES]
            if rope:
                blk = blk * cos + pltpu.roll(blk, LANES - 16, 1) * sa + pltpu.roll(blk, 16, 1) * sb
            if scale != 1.0:
                blk = blk * scale
            lo = (c * gpc + gi) * LANES
            o_ref[:, lo:lo + LANES] = blk.astype(BF16)


def proj_in(h, mod, nw, w, rope_tabs, groups, n_lat_tiles):
    b, n, d = h.shape
    dout = w.shape[1]
    tm = ROW_TILE
    chunk = 512 if dout % 512 == 0 else 256
    cos, sa, sb = rope_tabs
    tab_spec = pl.BlockSpec((tm, LANES), lambda bi, i: (i, 0))
    return pl.pallas_call(
        functools.partial(_proj_kernel, groups=groups, chunk=chunk),
        grid=(b, n // tm),
        in_specs=[
            pl.BlockSpec((None, tm, d), lambda bi, i: (bi, i, 0)),
            pl.BlockSpec((None, None, 2, d), lambda bi, i: (bi, jnp.where(i < n_lat_tiles, 1, 0), 0, 0)),
            pl.BlockSpec((1, d), lambda bi, i: (0, 0)),
            pl.BlockSpec((d, dout), lambda bi, i: (0, 0)),
            tab_spec, tab_spec, tab_spec,
        ],
        out_specs=pl.BlockSpec((None, tm, dout), lambda bi, i: (bi, i, 0)),
        out_shape=jax.ShapeDtypeStruct((b, n, dout), BF16),
        compiler_params=_cparams(("arbitrary", "arbitrary")),
        name="proj_in",
    )(h, mod, nw.reshape(1, d), w, cos, sa, sb)


def rope_tables(n_lat, n_ctx):
    f = DA_DIM // 4
    inv = ROPE_BASE ** (-jnp.arange(f, dtype=F32) / f)
    t = jnp.arange(n_lat)
    row = (t // GRID_W).astype(F32)
    col = (t % GRID_W).astype(F32)
    ang_r = row[:, None] * inv[None, :]
    ang_c = col[:, None] * inv[None, :]
    ang = jnp.concatenate([ang_r, ang_r, ang_c, ang_c], axis=-1)
    ang = jnp.concatenate([ang, jnp.zeros((n_ctx, 4 * f), F32)], axis=0)
    cos = jnp.cos(ang)
    sin = jnp.sin(ang)
    first = (jnp.arange(4 * f) % (2 * f)) < f
    sa = jnp.where(first[None, :], -sin, 0.0)
    sb = jnp.where(first[None, :], 0.0, sin)
    rep = LANES // (4 * f)
    return tuple(jnp.tile(x, (1, rep)) for x in (cos, sa, sb))


def _dattn_kernel(q_ref, k_ref, v_ref, lam_ref, nw_ref, o_ref, acc_sc, sa_sc, sb_sc, *, tk, lam_init):
    tq = q_ref.shape[0]
    n_chunks = k_ref.shape[0] // tk
    assert n_chunks % 2 == 1
    q = q_ref[...]
    lane = lax.broadcasted_iota(jnp.int32, q.shape, 1)
    zero = jnp.zeros_like(q)
    qs = (jnp.where(lane < DA_DIM, q, zero), jnp.where(lane >= DA_DIM, q, zero))
    acc_sc[...] = jnp.zeros(acc_sc.shape, F32)

    def scores(c, s_sc):
        k = k_ref[pl.ds(pl.multiple_of(c * tk, tk), tk), :]
        mx = []
        for sub in range(2):
            s = _dot_nt(k, qs[sub])
            s_sc[sub] = s
            mx.append(jnp.max(s, axis=0, keepdims=True))
        return tuple(mx)

    def absorb(c, s_sc, mx, ms, ls):
        v = v_ref[pl.ds(pl.multiple_of(c * tk, tk), tk), :]
        new_m, new_l = [], []
        for sub in range(2):
            m_new = jnp.maximum(ms[sub], mx[sub])
            alpha = jnp.exp2(ms[sub] - m_new)
            p = jnp.exp2(s_sc[sub] - m_new)
            new_l.append(alpha * ls[sub] + jnp.sum(p, axis=0, keepdims=True))
            acc_sc[sub] = alpha * acc_sc[sub] + _dot_tn(v, p.astype(BF16))
            new_m.append(m_new)
        return tuple(new_m), tuple(new_l)

    def body(j, carry):
        mx_a, ms, ls = carry
        ms, ls = absorb(2 * j, sa_sc, mx_a, ms, ls)
        mx_b = scores(2 * j + 1, sb_sc)
        ms, ls = absorb(2 * j + 1, sb_sc, mx_b, ms, ls)
        mx_a = scores(2 * j + 2, sa_sc)
        return mx_a, ms, ls

    m0 = jnp.full((1, tq), -jnp.inf, F32)
    l0 = jnp.zeros((1, tq), F32)
    mx_a, ms, ls = lax.fori_loop(0, n_chunks // 2, body, (scores(0, sa_sc), (m0, m0), (l0, l0)))
    _, ls = absorb(n_chunks - 1, sa_sc, mx_a, ms, ls)

    lp = lam_ref[...]
    lam = (jnp.exp(jnp.sum(lp[0:1] * lp[1:2], axis=-1, keepdims=True))
           - jnp.exp(jnp.sum(lp[2:3] * lp[3:4], axis=-1, keepdims=True))) + lam_init
    o = acc_sc[0] * (1.0 / ls[0]) - lam * (acc_sc[1] * (1.0 / ls[1]))
    y = o * lax.rsqrt(jnp.mean(o * o, axis=0, keepdims=True) + EPS)
    y = (y * nw_ref[...]) * (1.0 - lam_init)
    o_ref[...] = y.T.astype(o_ref.dtype)


def _dattn_call(p0, lam_p, dnorm_w, lam_init, tq, tk, q_tiles, q_tile0, kv_rows, kv_tile0):
    b = p0.shape[0]
    hd = 2 * DA_DIM
    return pl.pallas_call(
        functools.partial(_dattn_kernel, tk=tk, lam_init=lam_init),
        grid=(b, DA_HEADS, q_tiles),
        in_specs=[
            pl.BlockSpec((None, tq, hd), lambda bi, h, i: (bi, i + q_tile0, E_DQ + h)),
            pl.BlockSpec((None, kv_rows, hd), lambda bi, h, i: (bi, kv_tile0, E_DK + h)),
            pl.BlockSpec((None, kv_rows, hd), lambda bi, h, i: (bi, kv_tile0, E_DV + h)),
            pl.BlockSpec((4, DA_DIM), lambda bi, h, i: (0, 0)),
            pl.BlockSpec((hd, 1), lambda bi, h, i: (0, 0)),
        ],
        out_specs=pl.BlockSpec((None, tq, hd), lambda bi, h, i: (bi, i, h)),
        out_shape=jax.ShapeDtypeStruct((b, q_tiles * tq, DA_HEADS * hd), BF16),
        scratch_shapes=[pltpu.VMEM((2, hd, tq), F32)] + [pltpu.VMEM((2, tk, tq), F32)] * 2,
        compiler_params=_cparams(("arbitrary", "arbitrary", "arbitrary")),
        name="diff_attn",
    )(p0, p0, p0, lam_p, dnorm_w.reshape(hd, 1))


def diff_attention(p0, lam_p, dnorm_w, lam_init, n_ctx):
    n = p0.shape[1]
    s = n - n_ctx
    tq = min(DATTN_TQ, s)
    tk = DATTN_TK if n % DATTN_TK == 0 else n_ctx
    assert s % tq == 0 and n % tk == 0 and s % n_ctx == 0
    lat = _dattn_call(p0, lam_p, dnorm_w, lam_init, tq, tk, s // tq, 0, n, 0)
    ctx = _dattn_call(p0, lam_p, dnorm_w, lam_init, n_ctx, n_ctx, 1, s // n_ctx, n_ctx, s // n_ctx)
    return jnp.concatenate([lat, ctx], axis=1)


def _ret_kernel(*refs, reverse, finalize):
    if finalize:
        q_ref, k_ref, v_ref, lg_ref, ob_ref, g_ref, nw_ref, o_ref, s_sc = refs
    else:
        q_ref, k_ref, v_ref, lg_ref, o_ref, s_sc = refs
    j = pl.program_id(1)
    c = q_ref.shape[0]

    @pl.when(j == 0)
    def _():
        s_sc[...] = jnp.zeros(s_sc.shape, F32)

    ri = lax.broadcasted_iota(jnp.int32, (c, c), 0)
    ci = lax.broadcasted_iota(jnp.int32, (c, c), 1)
    dist = (ci - ri) if reverse else (ri - ci)
    distf = jnp.maximum(dist, 0).astype(F32)
    pos = lax.broadcasted_iota(jnp.int32, (c, RET_QK), 0).astype(F32)
    q_exp = (c - pos) if reverse else (pos + 1.0)
    k_exp = pos if reverse else (c - 1.0 - pos)

    for h in range(RET_HEADS):
        x = lg_ref[h]
        lg = -jnp.log(1.0 + jnp.exp(-x))
        lgq = lg[:, :RET_QK]
        q = q_ref[:, h * RET_QK:(h + 1) * RET_QK].astype(F32)
        k = k_ref[:, h * RET_QK:(h + 1) * RET_QK].astype(F32)
        v = v_ref[:, h * RET_V:(h + 1) * RET_V]
        decay = jnp.where(dist >= 0, jnp.exp(lg[:, :1] * distf), 0.0)
        scores = _dot_nt(q.astype(BF16), k.astype(BF16)) * decay
        inner = _dot(scores.astype(BF16), v)
        q_dec = q * jnp.exp(lgq * q_exp)
        k_dec = k * jnp.exp(lgq * k_exp)
        state = s_sc[h]
        out = inner + _dot(q_dec.astype(BF16), state.astype(BF16))
        s_sc[h] = jnp.exp(lg * float(c)) * state + _dot_tn(k_dec.astype(BF16), v)
        if finalize:
            r = out + ob_ref[:, h * RET_V:(h + 1) * RET_V]
            y = r * lax.rsqrt(jnp.mean(r * r, axis=-1, keepdims=True) + EPS) * nw_ref[...]
            g = g_ref[:, h * RET_V:(h + 1) * RET_V].astype(F32)
            o_ref[:, h * RET_V:(h + 1) * RET_V] = (y * (g * _sigmoid(g))).astype(o_ref.dtype)
        else:
            o_ref[:, h * RET_V:(h + 1) * RET_V] = out


def retention(p0, decay_logit, rnorm_w):
    b, n, _ = p0.shape
    c = ROW_TILE
    nc = n // c
    logit = jnp.broadcast_to(decay_logit.astype(F32)[:, :, None, None], (2, RET_HEADS, 1, LANES))
    qw, vw = RET_HEADS * RET_QK, RET_HEADS * RET_V

    def specs(cmap):
        return [
            pl.BlockSpec((None, c, qw), lambda bi, j: (bi, cmap(j), E_RQ * LANES // qw)),
            pl.BlockSpec((None, c, qw), lambda bi, j: (bi, cmap(j), E_RK * LANES // qw)),
            pl.BlockSpec((None, c, vw), lambda bi, j: (bi, cmap(j), E_RV * LANES // vw)),
        ]

    lg_spec = lambda d: pl.BlockSpec((None, RET_HEADS, 1, LANES), lambda bi, j: (d, 0, 0, 0))
    scratch = [pltpu.VMEM((RET_HEADS, RET_QK, RET_V), F32)]
    bmap = lambda j: jnp.where(j == 0, nc - 1, nc - 1 - j)
    o_b = pl.pallas_call(
        functools.partial(_ret_kernel, reverse=True, finalize=False),
        grid=(b, nc),
        in_specs=specs(bmap) + [lg_spec(1)],
        out_specs=pl.BlockSpec((None, c, vw), lambda bi, j: (bi, bmap(j), 0)),
        out_shape=jax.ShapeDtypeStruct((b, n, vw), F32),
        scratch_shapes=scratch,
        compiler_params=_cparams(("arbitrary", "arbitrary")),
        name="retention_bwd",
    )(p0, p0, p0, logit)
    fmap = lambda j: jnp.where(j == 0, nc - 1, j - 1)
    return pl.pallas_call(
        functools.partial(_ret_kernel, reverse=False, finalize=True),
        grid=(b, nc),
        in_specs=specs(fmap) + [
            lg_spec(0),
            pl.BlockSpec((None, c, vw), lambda bi, j: (bi, fmap(j), 0)),
            pl.BlockSpec((None, c, vw), lambda bi, j: (bi, fmap(j), E_RG * LANES // vw)),
            pl.BlockSpec((1, RET_V), lambda bi, j: (0, 0)),
        ],
        out_specs=pl.BlockSpec((None, c, vw), lambda bi, j: (bi, fmap(j), 0)),
        out_shape=jax.ShapeDtypeStruct((b, n, vw), BF16),
        scratch_shapes=scratch,
        compiler_params=_cparams(("arbitrary", "arbitrary")),
        name="retention_fwd",
    )(p0, p0, p0, logit, o_b, p0, rnorm_w.reshape(1, RET_V))


def _wattn_kernel(q_ref, kc_ref, vc_ref, kp_ref, vp_ref, k0_ref, v0_ref, kn_ref, vn_ref, sink_ref, o_ref,
                  *, n_tiles, group):
    i = pl.program_id(1)
    tq = q_ref.shape[0]
    nc = kc_ref.shape[0]
    kcat = jnp.concatenate([kc_ref[...], kp_ref[...], k0_ref[...], kn_ref[...]], axis=0)
    vcat = jnp.concatenate([vc_ref[...], vp_ref[...], v0_ref[...], vn_ref[...]], axis=0)
    nk = nc + 2 * WINDOW + tq
    kk = lax.broadcasted_iota(jnp.int32, (nk, tq), 0)
    qq = lax.broadcasted_iota(jnp.int32, (nk, tq), 1)
    koff = kk - (nc + WINDOW)
    far = 4 * tq
    miss = jnp.where(koff < 0, jnp.where(i > 0, 0, far), jnp.where(koff >= tq, jnp.where(i < n_tiles - 1, 0, far), 0))
    dist = jnp.where(kk < nc, 0, jnp.abs(qq - koff) + miss)
    bias1 = jnp.where(dist <= WINDOW, 0.0, NEG_INF).astype(F32)
    bias = jnp.concatenate([bias1, bias1], axis=1)
    lane = lax.broadcasted_iota(jnp.int32, (tq, LANES), 1)
    lo_half = lane < SW_DIM
    for hk in range(SW_KV_HEADS):
        kh = kcat[:, hk * SW_DIM:(hk + 1) * SW_DIM]
        kdup = jnp.concatenate([kh, kh], axis=1)
        for pg in range(group // 2):
            pair = hk * (group // 2) + pg
            qp = q_ref[:, pair * LANES:(pair + 1) * LANES]
            zero = jnp.zeros_like(qp)
            qstack = jnp.concatenate([jnp.where(lo_half, qp, zero), jnp.where(lo_half, zero, qp)], axis=0)
            s = _dot_nt(kdup, qstack) + bias
            sink = jnp.concatenate([jnp.broadcast_to(sink_ref[2 * pair][:, :1], (1, tq)),
                                    jnp.broadcast_to(sink_ref[2 * pair + 1][:, :1], (1, tq))], axis=1)
            sink = sink * LOG2E
            m = jnp.maximum(jnp.max(s, axis=0, keepdims=True), sink)
            p = jnp.exp2(s - m)
            den = jnp.sum(p, axis=0, keepdims=True) + jnp.exp2(sink - m)
            ot = _dot_tn(vcat, p.astype(BF16)) * (1.0 / den)
            oth = ot[hk * SW_DIM:(hk + 1) * SW_DIM]
            both = jnp.concatenate([oth[:, :tq], oth[:, tq:]], axis=0)
            o_ref[:, pair * LANES:(pair + 1) * LANES] = both.T.astype(o_ref.dtype)


def window_attention(p1, sinks, n_ctx):
    b, n, _ = p1.shape
    tq = ROW_TILE
    assert n_ctx == tq and WINDOW <= tq
    n_tiles = (n - n_ctx) // tq
    group = SW_Q_HEADS // SW_KV_HEADS
    qw = SW_Q_HEADS * SW_DIM
    kvw = SW_KV_HEADS * SW_DIM
    assert kvw == LANES and group % 2 == 0
    k_col = qw // kvw
    v_col = k_col + 1
    sink_b = jnp.broadcast_to(sinks.astype(F32).reshape(SW_Q_HEADS, 1, 1), (SW_Q_HEADS, 1, LANES))

    def kv(col, rows, tile):
        return pl.BlockSpec((None, rows, kvw), lambda bi, i: (bi, tile(i), col))

    per = tq // WINDOW
    assert tq % WINDOW == 0
    prev = lambda i: jnp.maximum(i * per - 1, 0)
    cur = lambda i: i
    nxt = lambda i: jnp.minimum((i + 1) * per, n_tiles * per - 1)
    ctx = lambda i: n_tiles
    return pl.pallas_call(
        functools.partial(_wattn_kernel, n_tiles=n_tiles, group=group),
        grid=(b, n_tiles),
        in_specs=[
            pl.BlockSpec((None, tq, qw), lambda bi, i: (bi, i, 0)),
            kv(k_col, tq, ctx), kv(v_col, tq, ctx), kv(k_col, WINDOW, prev), kv(v_col, WINDOW, prev),
            kv(k_col, tq, cur), kv(v_col, tq, cur), kv(k_col, WINDOW, nxt), kv(v_col, WINDOW, nxt),
            pl.BlockSpec((SW_Q_HEADS, 1, LANES), lambda bi, i: (0, 0, 0)),
        ],
        out_specs=pl.BlockSpec((None, tq, qw), lambda bi, i: (bi, i, 0)),
        out_shape=jax.ShapeDtypeStruct((b, n - n_ctx, qw), BF16),
        compiler_params=_cparams(("arbitrary", "arbitrary")),
        name="window_attn",
    )(p1, p1, p1, p1, p1, p1, p1, p1, p1, sink_b)


def _post_kernel(*refs, n_pieces):
    pieces = refs[:n_pieces]
    w_ref, h_ref, mod_ref, nw_ref, rw_ref, rb_ref, ho_ref, f_ref, lg_ref = refs[n_pieces:]
    m = None
    lo = 0
    for p in pieces:
        wdt = p.shape[1]
        t = _dot(p[...], w_ref[lo:lo + wdt, :])
        m = t if m is None else m + t
        lo += wdt
    md = mod_ref[...]
    hn = h_ref[...] + md[0:1] * m
    ho_ref[...] = hn
    f = _norm_mod(hn, nw_ref[...], md[1:2], md[2:3]).astype(BF16)
    f_ref[...] = f
    lg_ref[...] = _dot(f, rw_ref[...]) + rb_ref[...]


def post_mixer(pieces, w_out, h, mod, nw, router_w, router_b, n_lat_tiles):
    b, r, _ = pieces[0].shape
    d = h.shape[2]
    tm = ROW_TILE
    rw = jnp.zeros((d, LANES), BF16).at[:, :N_EXPERTS].set(router_w.astype(BF16))
    rb = jnp.full((1, LANES), NEG_INF, F32).at[0, :N_EXPERTS].set(router_b.astype(F32))
    row = lambda bi, i: (bi, i, 0)
    const2 = lambda bi, i: (0, 0)
    outs = pl.pallas_call(
        functools.partial(_post_kernel, n_pieces=len(pieces)),
        grid=(b, r // tm),
        in_specs=[pl.BlockSpec((None, tm, p.shape[2]), row) for p in pieces] + [
            pl.BlockSpec(w_out.shape, const2),
            pl.BlockSpec((None, tm, d), row),
            pl.BlockSpec((None, None, 3, d), lambda bi, i: (bi, jnp.where(i < n_lat_tiles, 1, 0), 0, 0)),
            pl.BlockSpec((1, d), const2),
            pl.BlockSpec((d, LANES), const2),
            pl.BlockSpec((1, LANES), const2),
        ],
        out_specs=[pl.BlockSpec((None, tm, d), row), pl.BlockSpec((None, tm, d), row),
                   pl.BlockSpec((None, tm, LANES), row)],
        out_shape=[jax.ShapeDtypeStruct((b, r, d), F32), jax.ShapeDtypeStruct((b, r, d), BF16),
                   jax.ShapeDtypeStruct((b, r, LANES), F32)],
        compiler_params=_cparams(("arbitrary", "arbitrary")),
        name="post_mixer",
    )(*pieces, w_out, h, mod, nw.reshape(1, d), rw, rb)
    return outs


def _router_kernel(lg_ref, info_ref, tile_ref, cnt_ref, base_sc):
    i = pl.program_id(0)
    tm = lg_ref.shape[0]

    @pl.when(i == 0)
    def _():
        base_sc[...] = jnp.zeros(base_sc.shape, F32)

    l = lg_ref[...]
    lane = lax.broadcasted_iota(jnp.int32, l.shape, 1)
    lanef = lane.astype(F32)
    hots, vals, idxs = [], [], []
    for _ in range(TOP_K):
        m = jnp.max(l, axis=1, keepdims=True)
        idx = jnp.min(jnp.where(l == m, lanef, float(LANES)), axis=1, keepdims=True)
        hot = lanef == idx
        hots.append(hot)
        vals.append(m)
        idxs.append(idx)
        l = jnp.where(hot, -jnp.inf, l)
    es = [jnp.exp(v - vals[0]) for v in vals]
    tot = es[0] + es[1] + es[2] + es[3]
    sel = jnp.zeros(l.shape, F32)
    for hot in hots:
        sel = sel + jnp.where(hot, 1.0, 0.0)
    ri = lax.broadcasted_iota(jnp.int32, (tm, tm), 0)
    ci = lax.broadcasted_iota(jnp.int32, (tm, tm), 1)
    earlier = jnp.where(ri > ci, 1.0, 0.0).astype(BF16)
    before = _dot(earlier, sel.astype(BF16))
    cnt = jnp.sum(sel, axis=0, keepdims=True)
    run = jnp.floor((cnt + (RUN_ROWS - 1.0)) * (1.0 / RUN_ROWS)) * RUN_ROWS
    li = lax.broadcasted_iota(jnp.int32, (LANES, LANES), 0)
    lj = lax.broadcasted_iota(jnp.int32, (LANES, LANES), 1)
    lower = jnp.where(li < lj, 1.0, 0.0).astype(BF16)
    run_off = _dot(jnp.broadcast_to(run, (8, LANES)).astype(BF16), lower)[0:1]
    where_ = before + run_off
    info = jnp.zeros(l.shape, F32)
    for k in range(TOP_K):
        row = jnp.sum(jnp.where(hots[k], where_, 0.0), axis=1, keepdims=True)
        info = jnp.where(lane == k, idxs[k], info)
        info = jnp.where(lane == TOP_K + k, row, info)
        info = jnp.where(lane == 2 * TOP_K + k, es[k] / tot, info)
    info_ref[...] = info
    sub = lax.broadcasted_iota(jnp.int32, tile_ref.shape, 0)
    tile_ref[...] = jnp.where(sub == 0, cnt, jnp.where(sub == 1, base_sc[...], jnp.where(sub == 2, run_off, 0.0)))
    base_sc[...] = base_sc[...] + jnp.floor((cnt + (SLOT_ALIGN - 1.0)) * (1.0 / SLOT_ALIGN)) * SLOT_ALIGN
    cnt_ref[...] = jnp.broadcast_to(base_sc[...], cnt_ref.shape)


def route(logits):
    t = logits.shape[0]
    tm = ROUTE_TILE
    nt = t // tm
    info, tile, cnt = pl.pallas_call(
        _router_kernel,
        grid=(nt,),
        in_specs=[pl.BlockSpec((tm, LANES), lambda i: (i, 0))],
        out_specs=[pl.BlockSpec((tm, LANES), lambda i: (i, 0)), pl.BlockSpec((None, 8, LANES), lambda i: (i, 0, 0)),
                   pl.BlockSpec((8, LANES), lambda i: (0, 0))],
        out_shape=[jax.ShapeDtypeStruct((t, LANES), F32), jax.ShapeDtypeStruct((nt, 8, LANES), F32),
                   jax.ShapeDtypeStruct((8, LANES), F32)],
        scratch_shapes=[pltpu.VMEM((1, LANES), F32)],
        compiler_params=_cparams(("arbitrary",)),
        name="router",
    )(logits)
    counts = cnt[0, :N_EXPERTS].astype(jnp.int32)
    padded = ((counts + RUN_ROWS + MOE_TILE - 1) // MOE_TILE) * MOE_TILE
    pad_ends = jnp.cumsum(padded)
    pad_starts = pad_ends - padded
    worst = t * TOP_K + nt * N_EXPERTS * (SLOT_ALIGN - 1) + N_EXPERTS * (RUN_ROWS + MOE_TILE - 1)
    n_blocks = -(-worst // MOE_TILE)
    block_lo = jnp.arange(n_blocks, dtype=jnp.int32) * MOE_TILE
    block_expert = jnp.minimum(jnp.sum((pad_ends[None, :] <= block_lo[:, None]).astype(jnp.int32), axis=1),
                               N_EXPERTS - 1)
    n_used = (pad_ends[-1] // MOE_TILE).astype(jnp.int32).reshape(1)
    tcnt = tile[:, 0, :N_EXPERTS].astype(jnp.int32)
    slot0 = pad_starts[None, :] + tile[:, 1, :N_EXPERTS].astype(jnp.int32)
    runs = (tcnt + RUN_ROWS - 1) // RUN_ROWS
    roff = tile[:, 2, :N_EXPERTS].astype(jnp.int32)
    table = jnp.concatenate([slot0, runs, roff, jnp.zeros_like(roff)], axis=1).reshape(nt, 1, 4 * N_EXPERTS)
    return info, table, block_expert, n_used, n_blocks


def _run_copies(table_ref, make):
    def for_each_run(fn):
        def per_expert(e, carry):
            slot0 = table_ref[0, e]
            row0 = table_ref[0, 2 * N_EXPERTS + e]

            def one(j, c):
                fn(make(pl.multiple_of(slot0 + j * RUN_ROWS, SLOT_ALIGN),
                        pl.multiple_of(row0 + j * RUN_ROWS, RUN_ROWS)))
                return c

            return lax.fori_loop(0, table_ref[0, N_EXPERTS + e], one, carry)

        lax.fori_loop(0, N_EXPERTS, per_expert, 0)

    for_each_run(lambda cp: cp.start())
    for_each_run(lambda cp: cp.wait())


def _staging_onehots(info):
    tm = info.shape[0]
    col = lax.broadcasted_iota(jnp.int32, (tm, STAGE_ROWS), 1).astype(F32)
    return [col == info[:, TOP_K + k:TOP_K + k + 1] for k in range(TOP_K)]


def _dispatch_kernel(table_ref, info_ref, f_ref, xin_ref, xbuf_ref, stage, sem):
    del xin_ref
    hots = _staging_onehots(info_ref[...])
    sel = jnp.logical_or(jnp.logical_or(hots[0], hots[1]), jnp.logical_or(hots[2], hots[3]))
    place = jnp.where(sel, 1.0, 0.0).astype(BF16)
    half = f_ref.shape[1] // 2
    for c in range(half // STAGE_COLS):
        lo = c * STAGE_COLS
        stage[:, lo:lo + STAGE_COLS] = _pack_halves(_dot_tn(place, f_ref[:, lo:lo + STAGE_COLS]),
                                                    _dot_tn(place, f_ref[:, half + lo:half + lo + STAGE_COLS]))

    def make(slot, row):
        return pltpu.make_async_copy(stage.at[pl.ds(row, RUN_ROWS), :], xbuf_ref.at[pl.ds(slot, RUN_ROWS), :], sem)

    _run_copies(table_ref, make)


def dispatch(f, info, table, n_slots):
    t, d = f.shape
    tm = ROUTE_TILE
    x0 = jnp.zeros((n_slots, d // 2), jnp.uint32)
    return pl.pallas_call(
        _dispatch_kernel,
        grid=(t // tm,),
        in_specs=[
            pl.BlockSpec((None, 1, 4 * N_EXPERTS), lambda i: (i, 0, 0), memory_space=pltpu.SMEM),
            pl.BlockSpec((tm, LANES), lambda i: (i, 0)),
            pl.BlockSpec((tm, d), lambda i: (i, 0)),
            pl.BlockSpec(memory_space=pl.ANY),
        ],
        out_specs=pl.BlockSpec(memory_space=pl.ANY),
        out_shape=jax.ShapeDtypeStruct((n_slots, d // 2), jnp.uint32),
        scratch_shapes=[pltpu.VMEM((STAGE_ROWS, d // 2), jnp.uint32), pltpu.SemaphoreType.DMA(())],
        input_output_aliases={3: 0},
        compiler_params=_cparams(("arbitrary",)),
        name="dispatch",
    )(table, info, f, x0)


def _ffn_kernel(be_ref, nb_ref, x_ref, w1_ref, b1_ref, w2_ref, b2_ref, o_ref, w1_sc, w2_sc):
    i = pl.program_id(0)
    used = i < nb_ref[0]
    fresh = jnp.logical_or(i == 0, be_ref[i] != be_ref[jnp.maximum(i - 1, 0)])

    @pl.when(jnp.logical_and(used, fresh))
    def _():
        w1_sc[...] = w1_ref[...].astype(BF16)
        w2_sc[...] = w2_ref[...].astype(BF16)

    @pl.when(used)
    def _():
        x = _unpack_pairs(x_ref[...]).astype(BF16)
        dff = w2_sc.shape[0]
        acc = None
        for c in range(dff // FF_CHUNK):
            lo = c * FF_CHUNK
            hg = _dot(x, w1_sc[:, lo:lo + FF_CHUNK]) + b1_ref[:, lo:lo + FF_CHUNK]
            hl = _dot(x, w1_sc[:, dff + lo:dff + lo + FF_CHUNK]) + b1_ref[:, dff + lo:dff + lo + FF_CHUNK]
            glu = jnp.minimum(hg, SWIGLU_LIMIT)
            lin = jnp.clip(hl, -SWIGLU_LIMIT, SWIGLU_LIMIT)
            act = glu * _sigmoid(SWIGLU_ALPHA * glu) * (lin + 1.0)
            t = _dot(act.astype(BF16), w2_sc[lo:lo + FF_CHUNK, :])
            acc = t if acc is None else acc + t
        o_ref[...] = _pack_pairs(acc + b2_ref[...])

    @pl.when(jnp.logical_not(used))
    def _():
        o_ref[...] = jnp.zeros(o_ref.shape, o_ref.dtype)


def expert_ffn(x_buf, block_expert, n_used, layer, w1, b1, w2, b2):
    p, w = x_buf.shape
    nl, ne, d, f2 = w1.shape
    dff = w2.shape[2]
    tm = MOE_TILE
    grid_spec = pltpu.PrefetchScalarGridSpec(
        num_scalar_prefetch=2,
        grid=(p // tm,),
        in_specs=[
            pl.BlockSpec((tm, w), lambda i, be, nb: (i, 0)),
            pl.BlockSpec((None, None, d, f2), lambda i, be, nb: (layer, be[i], 0, 0)),
            pl.BlockSpec((None, None, 1, f2), lambda i, be, nb: (layer, be[i], 0, 0)),
            pl.BlockSpec((None, None, dff, d), lambda i, be, nb: (layer, be[i], 0, 0)),
            pl.BlockSpec((None, None, 1, d), lambda i, be, nb: (layer, be[i], 0, 0)),
        ],
        out_specs=pl.BlockSpec((tm, w), lambda i, be, nb: (i, 0)),
        scratch_shapes=[pltpu.VMEM((d, f2), BF16), pltpu.VMEM((dff, d), BF16)],
    )
    return pl.pallas_call(
        _ffn_kernel,
        grid_spec=grid_spec,
        out_shape=jax.ShapeDtypeStruct((p, w), jnp.uint32),
        compiler_params=pltpu.CompilerParams(dimension_semantics=("arbitrary",), vmem_limit_bytes=FFN_VMEM_LIMIT),
        name="expert_ffn",
    )(block_expert, n_used, x_buf, w1, b1.reshape(nl, ne, 1, f2), w2, b2.reshape(nl, ne, 1, d))


def _combine_kernel(*refs, final):
    if final:
        table_ref, h_ref, info_ref, g2a_ref, g2b_ref, nw_ref, ybuf_ref, o_ref, stage, sem = refs
    else:
        table_ref, h_ref, info_ref, g2a_ref, g2b_ref, ybuf_ref, o_ref, stage, sem = refs

    @pl.when(pl.program_id(0) == 0)
    def _():
        stage[...] = jnp.zeros(stage.shape, stage.dtype)

    def make(slot, row):
        return pltpu.make_async_copy(ybuf_ref.at[pl.ds(slot, RUN_ROWS), :], stage.at[pl.ds(row, RUN_ROWS), :], sem)

    _run_copies(table_ref, make)

    info = info_ref[...]
    hots = _staging_onehots(info)
    wgt = jnp.zeros(hots[0].shape, F32)
    for k in range(TOP_K):
        wgt = jnp.where(hots[k], info[:, 2 * TOP_K + k:2 * TOP_K + k + 1], wgt)
    w_hi = wgt.astype(BF16)
    w_lo = (wgt - w_hi.astype(F32)).astype(BF16)
    half = h_ref.shape[1] // 2
    lows, highs = [], []
    for c in range(half // STAGE_COLS):
        y_lo, y_hi = _unpack_halves(stage[:, c * STAGE_COLS:(c + 1) * STAGE_COLS])
        y_lo = y_lo.astype(BF16)
        y_hi = y_hi.astype(BF16)
        lows.append(_dot(w_hi, y_lo) + _dot(w_lo, y_lo))
        highs.append(_dot(w_hi, y_hi) + _dot(w_lo, y_hi))
    acc = jnp.concatenate(lows + highs, axis=1)
    sub = h_ref.shape[0] // 2
    g2 = jnp.concatenate([jnp.broadcast_to(g2a_ref[...], (sub, acc.shape[1])),
                          jnp.broadcast_to(g2b_ref[...], (sub, acc.shape[1]))], axis=0)
    hn = h_ref[...] + g2 * acc
    if final:
        y = hn * lax.rsqrt(jnp.mean(hn * hn, axis=-1, keepdims=True) + EPS)
        hn = y * nw_ref[...]
    o_ref[...] = hn


def combine(h, y_buf, info, table, g2, n_lat_tiles, final_w=None):
    b, r, d = h.shape
    tm = ROUTE_TILE
    rows_per_batch = r // ROW_TILE

    def g2_spec(j):
        def index(i):
            s = 2 * i + j
            return (s // rows_per_batch, jnp.where(s % rows_per_batch < n_lat_tiles, 1, 0), 0, 0)
        return pl.BlockSpec((None, None, 1, d), index)

    final = final_w is not None
    in_specs = [
        pl.BlockSpec((None, 1, 4 * N_EXPERTS), lambda i: (i, 0, 0), memory_space=pltpu.SMEM),
        pl.BlockSpec((tm, d), lambda i: (i, 0)),
        pl.BlockSpec((tm, LANES), lambda i: (i, 0)),
        g2_spec(0), g2_spec(1),
    ]
    args = [table, h.reshape(b * r, d), info, g2, g2]
    if final:
        in_specs.append(pl.BlockSpec((1, d), lambda i: (0, 0)))
        args.append(final_w.reshape(1, d))
    in_specs.append(pl.BlockSpec(memory_space=pl.ANY))
    args.append(y_buf)
    out = pl.pallas_call(
        functools.partial(_combine_kernel, final=final),
        grid=(b * r // tm,),
        in_specs=in_specs,
        out_specs=pl.BlockSpec((tm, d), lambda i: (i, 0)),
        out_shape=jax.ShapeDtypeStruct((b * r, d), F32),
        scratch_shapes=[pltpu.VMEM((STAGE_ROWS, d // 2), jnp.uint32), pltpu.SemaphoreType.DMA(())],
        compiler_params=_cparams(("arbitrary",)),
        name="combine",
    )(*args)
    return out.reshape(b, r, d)


def moe_layer(h, f, logits, g2, n_lat_tiles, layer, w1, b1, w2, b2, final_w=None):
    b, r, d = h.shape
    t = b * r
    info, table, block_expert, n_used, n_blocks = route(logits.reshape(t, LANES))
    x_buf = dispatch(f.reshape(t, d), info, table, n_blocks * MOE_TILE)
    y_buf = expert_ffn(x_buf, block_expert, n_used, layer, w1, b1, w2, b2)
    return combine(h, y_buf, info, table, g2, n_lat_tiles, final_w)


def _mod_table(rows, n_batch):
    lat = jnp.stack([r[:n_batch] for r in rows], axis=1)
    ctx = jnp.stack([r[n_batch] for r in rows], axis=0)
    return jnp.stack([jnp.broadcast_to(ctx[None], lat.shape), lat], axis=1)


def kernel(x, c, ctx, c_ctx, mod_w, mod_b, norm1_w, norm2_w, even_w_in, even_w_out, diff_lam, diff_norm_w,
           ret_decay_logit, ret_norm_w, odd_w_qkv, odd_w_out, odd_sinks, router_w, router_b,
           moe_w1, moe_b1, moe_w2, moe_b2, final_norm_w):
    b, s, d = x.shape
    n_ctx = ctx.shape[1]
    n_lat_tiles = s // ROW_TILE
    assert n_ctx == ROW_TILE and s % ROW_TILE == 0 and b + 1 <= 8

    cvec = jnp.zeros((8, d), F32).at[:b].set(c).at[b].set(c_ctx)
    mods = adaln_all(cvec, mod_w, mod_b)
    rope_tabs = rope_tables(s, n_ctx)

    sh1, sc1, g1, sh2, sc2, g2 = jnp.split(mods[0], 6, axis=-1)
    h = jnp.concatenate([x, ctx], axis=1)
    groups0 = ((True, 0.125 * LOG2E),) * 4 + ((True, 1.0),) * 4 + ((False, 1.0),) * 4 \
        + ((False, 0.125),) * 2 + ((False, 1.0),) * 10
    p0 = proj_in(h, _mod_table([sh1, sc1], b), norm1_w[0], even_w_in[0].astype(BF16), rope_tabs, groups0,
                 n_lat_tiles)
    lam_init = 0.8 - 0.6 * math.exp(-0.3 * 0)
    d_out = diff_attention(p0, diff_lam[0], diff_norm_w[0], lam_init, n_ctx)
    r_out = retention(p0, ret_decay_logit[0], ret_norm_w[0])
    h, f, logits = post_mixer([d_out, r_out], even_w_out[0].astype(BF16), h, _mod_table([g1, sh2, sc2], b),
                              norm2_w[0], router_w[0], router_b[0], n_lat_tiles)
    h = moe_layer(h, f, logits, _mod_table([g2], b), n_lat_tiles, 0, moe_w1, moe_b1, moe_w2, moe_b2)

    sh1, sc1, g1, sh2, sc2, g2 = jnp.split(mods[1], 6, axis=-1)
    groups1 = ((True, 0.125 * LOG2E),) * 8 + ((True, 1.0),) + ((False, 1.0),)
    p1 = proj_in(h, _mod_table([sh1, sc1], b), norm1_w[1], odd_w_qkv[0].astype(BF16), rope_tabs, groups1,
                 n_lat_tiles)
    a_out = window_attention(p1, odd_sinks[0], n_ctx)
    h_lat, f, logits = post_mixer([a_out], odd_w_out[0].astype(BF16), h, _mod_table([g1, sh2, sc2], b),
                                  norm2_w[1], router_w[1], router_b[1], n_lat_tiles)
    return moe_layer(h_lat, f, logits, _mod_table([g2], b), n_lat_tiles, 1, moe_w1, moe_b1, moe_w2, moe_b2,
                     final_w=final_norm_w)
```

```python
import functools
import math

import jax
import jax.numpy as jnp
from jax import lax
from jax.experimental import pallas as pl
from jax.experimental.pallas import tpu as pltpu

F32 = jnp.float32
BF16 = jnp.bfloat16

GRID_W = 64
EPS = 1e-6
ROPE_BASE = 10000.0
NEG_INF = -1e30
LOG2E = math.log2(math.e)

DA_HEADS = 4
DA_DIM = 64
RET_HEADS = 4
RET_QK = 64
RET_V = 128
SW_Q_HEADS = 16
SW_KV_HEADS = 2
SW_DIM = 64
WINDOW = 128
N_EXPERTS = 32
TOP_K = 4
SWIGLU_ALPHA = 1.702
SWIGLU_LIMIT = 7.0

LANES = 128
ROW_TILE = 256
MOE_TILE = 512
DATTN_TQ = 1024
DATTN_TK = 768
FF_CHUNK = 512
ROUTE_TILE = 512
SLOT_ALIGN = 8
RUN_ROWS = 16
STAGE_ROWS = -(-(ROUTE_TILE * TOP_K + N_EXPERTS * (RUN_ROWS - 1)) // LANES) * LANES
STAGE_COLS = 256
VMEM_LIMIT = 48 * 1024 * 1024
FFN_VMEM_LIMIT = 56 * 1024 * 1024

E_DQ, E_DK, E_DV, E_RQ, E_RK, E_RV, E_RG = 0, 4, 8, 12, 14, 16, 20


def _cparams(sem):
    return pltpu.CompilerParams(dimension_semantics=sem, vmem_limit_bytes=VMEM_LIMIT)


def _dot(a, b):
    return jnp.dot(a, b, preferred_element_type=F32)


def _dot_nt(a, b):
    return lax.dot_general(a, b, (((1,), (1,)), ((), ())), preferred_element_type=F32)


def _dot_tn(a, b):
    return lax.dot_general(a, b, (((0,), (0,)), ((), ())), preferred_element_type=F32)


def _sigmoid(x):
    return 1.0 / (1.0 + jnp.exp(-x))


def _pack_halves(lo, hi):
    lo = lax.bitcast_convert_type(lo.astype(BF16).astype(F32), jnp.uint32)
    hi = lax.bitcast_convert_type(hi.astype(BF16).astype(F32), jnp.uint32)
    return (hi & jnp.uint32(0xFFFF0000)) | (lo >> 16)


def _pack_pairs(x):
    w = x.shape[1] // 2
    return _pack_halves(x[:, :w], x[:, w:])


def _unpack_halves(u):
    return (lax.bitcast_convert_type(u << 16, F32), lax.bitcast_convert_type(u & jnp.uint32(0xFFFF0000), F32))


def _unpack_pairs(u):
    return jnp.concatenate(_unpack_halves(u), axis=1)


def _adaln_kernel(c_ref, w_ref, b_ref, o_ref):
    c = c_ref[...]
    s = c * _sigmoid(c)
    o_ref[...] = _dot(s.astype(BF16), w_ref[...].astype(BF16)) + b_ref[...]


def adaln_all(cvec, mod_w, mod_b):
    depth, d, d6 = mod_w.shape
    tn = 1024
    return pl.pallas_call(
        _adaln_kernel,
        grid=(depth, d6 // tn),
        in_specs=[
            pl.BlockSpec((8, d), lambda l, j: (0, 0)),
            pl.BlockSpec((None, d, tn), lambda l, j: (l, 0, j)),
            pl.BlockSpec((None, 1, tn), lambda l, j: (l, 0, j)),
        ],
        out_specs=pl.BlockSpec((None, 8, tn), lambda l, j: (l, 0, j)),
        out_shape=jax.ShapeDtypeStruct((depth, 8, d6), F32),
        compiler_params=_cparams(("arbitrary", "arbitrary")),
        name="adaln",
    )(cvec, mod_w, mod_b.reshape(depth, 1, d6))


def _norm_mod(x, nw, shift, scale):
    y = x * lax.rsqrt(jnp.mean(x * x, axis=-1, keepdims=True) + EPS)
    return (y * nw) * (1.0 + scale) + shift


def _proj_kernel(h_ref, mod_ref, nw_ref, w_ref, cos_ref, sa_ref, sb_ref, o_ref, *, groups, chunk):
    m = mod_ref[...]
    a = _norm_mod(h_ref[...], nw_ref[...], m[0:1], m[1:2]).astype(BF16)
    cos = cos_ref[...]
    sa = sa_ref[...]
    sb = sb_ref[...]
    gpc = chunk // LANES
    for c in range(len(groups) // gpc):
        acc = _dot(a, w_ref[:, c * chunk:(c + 1) * chunk])
        for gi in range(gpc):
            rope, scale = groups[c * gpc + gi]
            blk = acc[:, gi * LANES:(gi + 1) * LANES]
            if rope:
                blk = blk * cos + pltpu.roll(blk, LANES - 16, 1) * sa + pltpu.roll(blk, 16, 1) * sb
            if scale != 1.0:
                blk = blk * scale
            lo = (c * gpc + gi) * LANES
            o_ref[:, lo:lo + LANES] = blk.astype(BF16)


def proj_in(h, mod, nw, w, rope_tabs, groups, n_lat_tiles):
    b, n, d = h.shape
    dout = w.shape[1]
    tm = ROW_TILE
    chunk = 512 if dout % 512 == 0 else 256
    cos, sa, sb = rope_tabs
    tab_spec = pl.BlockSpec((tm, LANES), lambda bi, i: (i, 0))
    return pl.pallas_call(
        functools.partial(_proj_kernel, groups=groups, chunk=chunk),
        grid=(b, n // tm),
        in_specs=[
            pl.BlockSpec((None, tm, d), lambda bi, i: (bi, i, 0)),
            pl.BlockSpec((None, None, 2, d), lambda bi, i: (bi, jnp.where(i < n_lat_tiles, 1, 0), 0, 0)),
            pl.BlockSpec((1, d), lambda bi, i: (0, 0)),
            pl.BlockSpec((d, dout), lambda bi, i: (0, 0)),
            tab_spec, tab_spec, tab_spec,
        ],
        out_specs=pl.BlockSpec((None, tm, dout), lambda bi, i: (bi, i, 0)),
        out_shape=jax.ShapeDtypeStruct((b, n, dout), BF16),
        compiler_params=_cparams(("arbitrary", "arbitrary")),
        name="proj_in",
    )(h, mod, nw.reshape(1, d), w, cos, sa, sb)


def rope_tables(n_lat, n_ctx):
    f = DA_DIM // 4
    inv = ROPE_BASE ** (-jnp.arange(f, dtype=F32) / f)
    t = jnp.arange(n_lat)
    row = (t // GRID_W).astype(F32)
    col = (t % GRID_W).astype(F32)
    ang_r = row[:, None] * inv[None, :]
    ang_c = col[:, None] * inv[None, :]
    ang = jnp.concatenate([ang_r, ang_r, ang_c, ang_c], axis=-1)
    ang = jnp.concatenate([ang, jnp.zeros((n_ctx, 4 * f), F32)], axis=0)
    cos = jnp.cos(ang)
    sin = jnp.sin(ang)
    first = (jnp.arange(4 * f) % (2 * f)) < f
    sa = jnp.where(first[None, :], -sin, 0.0)
    sb = jnp.where(first[None, :], 0.0, sin)
    rep = LANES // (4 * f)
    return tuple(jnp.tile(x, (1, rep)) for x in (cos, sa, sb))


def _dattn_kernel(q_ref, k_ref, v_ref, lam_ref, nw_ref, o_ref, acc_sc, sa_sc, sb_sc, *, tk, lam_init):
    tq = q_ref.shape[0]
    n_chunks = k_ref.shape[0] // tk
    assert n_chunks % 2 == 1
    q = q_ref[...]
    lane = lax.broadcasted_iota(jnp.int32, q.shape, 1)
    zero = jnp.zeros_like(q)
    qs = (jnp.where(lane < DA_DIM, q, zero), jnp.where(lane >= DA_DIM, q, zero))
    acc_sc[...] = jnp.zeros(acc_sc.shape, F32)

    def scores(c, s_sc):
        k = k_ref[pl.ds(pl.multiple_of(c * tk, tk), tk), :]
        mx = []
        for sub in range(2):
            s = _dot_nt(k, qs[sub])
            s_sc[sub] = s
            mx.append(jnp.max(s, axis=0, keepdims=True))
        return tuple(mx)

    def absorb(c, s_sc, mx, ms, ls):
        v = v_ref[pl.ds(pl.multiple_of(c * tk, tk), tk), :]
        new_m, new_l = [], []
        for sub in range(2):
            m_new = jnp.maximum(ms[sub], mx[sub])
            alpha = jnp.exp2(ms[sub] - m_new)
            p = jnp.exp2(s_sc[sub] - m_new)
            new_l.append(alpha * ls[sub] + jnp.sum(p, axis=0, keepdims=True))
            acc_sc[sub] = alpha * acc_sc[sub] + _dot_tn(v, p.astype(BF16))
            new_m.append(m_new)
        return tuple(new_m), tuple(new_l)

    def body(j, carry):
        mx_a, ms, ls = carry
        ms, ls = absorb(2 * j, sa_sc, mx_a, ms, ls)
        mx_b = scores(2 * j + 1, sb_sc)
        ms, ls = absorb(2 * j + 1, sb_sc, mx_b, ms, ls)
        mx_a = scores(2 * j + 2, sa_sc)
        return mx_a, ms, ls

    m0 = jnp.full((1, tq), -jnp.inf, F32)
    l0 = jnp.zeros((1, tq), F32)
    mx_a, ms, ls = lax.fori_loop(0, n_chunks // 2, body, (scores(0, sa_sc), (m0, m0), (l0, l0)))
    _, ls = absorb(n_chunks - 1, sa_sc, mx_a, ms, ls)

    lp = lam_ref[...]
    lam = (jnp.exp(jnp.sum(lp[0:1] * lp[1:2], axis=-1, keepdims=True))
           - jnp.exp(jnp.sum(lp[2:3] * lp[3:4], axis=-1, keepdims=True))) + lam_init
    o = acc_sc[0] * (1.0 / ls[0]) - lam * (acc_sc[1] * (1.0 / ls[1]))
    y = o * lax.rsqrt(jnp.mean(o * o, axis=0, keepdims=True) + EPS)
    y = (y * nw_ref[...]) * (1.0 - lam_init)
    o_ref[...] = y.T.astype(o_ref.dtype)


def _dattn_call(p0, lam_p, dnorm_w, lam_init, tq, tk, q_tiles, q_tile0, kv_rows, kv_tile0):
    b = p0.shape[0]
    hd = 2 * DA_DIM
    return pl.pallas_call(
        functools.partial(_dattn_kernel, tk=tk, lam_init=lam_init),
        grid=(b, DA_HEADS, q_tiles),
        in_specs=[
            pl.BlockSpec((None, tq, hd), lambda bi, h, i: (bi, i + q_tile0, E_DQ + h)),
            pl.BlockSpec((None, kv_rows, hd), lambda bi, h, i: (bi, kv_tile0, E_DK + h)),
            pl.BlockSpec((None, kv_rows, hd), lambda bi, h, i: (bi, kv_tile0, E_DV + h)),
            pl.BlockSpec((4, DA_DIM), lambda bi, h, i: (0, 0)),
            pl.BlockSpec((hd, 1), lambda bi, h, i: (0, 0)),
        ],
        out_specs=pl.BlockSpec((None, tq, hd), lambda bi, h, i: (bi, i, h)),
        out_shape=jax.ShapeDtypeStruct((b, q_tiles * tq, DA_HEADS * hd), BF16),
        scratch_shapes=[pltpu.VMEM((2, hd, tq), F32)] + [pltpu.VMEM((2, tk, tq), F32)] * 2,
        compiler_params=_cparams(("arbitrary", "arbitrary", "arbitrary")),
        name="diff_attn",
    )(p0, p0, p0, lam_p, dnorm_w.reshape(hd, 1))


def diff_attention(p0, lam_p, dnorm_w, lam_init, n_ctx):
    n = p0.shape[1]
    s = n - n_ctx
    tq = min(DATTN_TQ, s)
    tk = DATTN_TK if n % DATTN_TK == 0 else n_ctx
    assert s % tq == 0 and n % tk == 0 and s % n_ctx == 0
    lat = _dattn_call(p0, lam_p, dnorm_w, lam_init, tq, tk, s // tq, 0, n, 0)
    ctx = _dattn_call(p0, lam_p, dnorm_w, lam_init, n_ctx, n_ctx, 1, s // n_ctx, n_ctx, s // n_ctx)
    return jnp.concatenate([lat, ctx], axis=1)


def _ret_kernel(*refs, reverse, finalize):
    if finalize:
        q_ref, k_ref, v_ref, lg_ref, ob_ref, g_ref, nw_ref, o_ref, s_sc = refs
    else:
        q_ref, k_ref, v_ref, lg_ref, o_ref, s_sc = refs
    j = pl.program_id(1)
    c = q_ref.shape[0]

    @pl.when(j == 0)
    def _():
        s_sc[...] = jnp.zeros(s_sc.shape, F32)

    ri = lax.broadcasted_iota(jnp.int32, (c, c), 0)
    ci = lax.broadcasted_iota(jnp.int32, (c, c), 1)
    dist = (ci - ri) if reverse else (ri - ci)
    distf = jnp.maximum(dist, 0).astype(F32)
    pos = lax.broadcasted_iota(jnp.int32, (c, RET_QK), 0).astype(F32)
    q_exp = (c - pos) if reverse else (pos + 1.0)
    k_exp = pos if reverse else (c - 1.0 - pos)

    for h in range(RET_HEADS):
        x = lg_ref[h]
        lg = -jnp.log(1.0 + jnp.exp(-x))
        lgq = lg[:, :RET_QK]
        q = q_ref[:, h * RET_QK:(h + 1) * RET_QK].astype(F32)
        k = k_ref[:, h * RET_QK:(h + 1) * RET_QK].astype(F32)
        v = v_ref[:, h * RET_V:(h + 1) * RET_V]
        decay = jnp.where(dist >= 0, jnp.exp(lg[:, :1] * distf), 0.0)
        scores = _dot_nt(q.astype(BF16), k.astype(BF16)) * decay
        inner = _dot(scores.astype(BF16), v)
        q_dec = q * jnp.exp(lgq * q_exp)
        k_dec = k * jnp.exp(lgq * k_exp)
        state = s_sc[h]
        out = inner + _dot(q_dec.astype(BF16), state.astype(BF16))
        s_sc[h] = jnp.exp(lg * float(c)) * state + _dot_tn(k_dec.astype(BF16), v)
        if finalize:
            r = out + ob_ref[:, h * RET_V:(h + 1) * RET_V]
            y = r * lax.rsqrt(jnp.mean(r * r, axis=-1, keepdims=True) + EPS) * nw_ref[...]
            g = g_ref[:, h * RET_V:(h + 1) * RET_V].astype(F32)
            o_ref[:, h * RET_V:(h + 1) * RET_V] = (y * (g * _sigmoid(g))).astype(o_ref.dtype)
        else:
            o_ref[:, h * RET_V:(h + 1) * RET_V] = out


def retention(p0, decay_logit, rnorm_w):
    b, n, _ = p0.shape
    c = ROW_TILE
    nc = n // c
    logit = jnp.broadcast_to(decay_logit.astype(F32)[:, :, None, None], (2, RET_HEADS, 1, LANES))
    qw, vw = RET_HEADS * RET_QK, RET_HEADS * RET_V

    def specs(cmap):
        return [
            pl.BlockSpec((None, c, qw), lambda bi, j: (bi, cmap(j), E_RQ * LANES // qw)),
            pl.BlockSpec((None, c, qw), lambda bi, j: (bi, cmap(j), E_RK * LANES // qw)),
            pl.BlockSpec((None, c, vw), lambda bi, j: (bi, cmap(j), E_RV * LANES // vw)),
        ]

    lg_spec = lambda d: pl.BlockSpec((None, RET_HEADS, 1, LANES), lambda bi, j: (d, 0, 0, 0))
    scratch = [pltpu.VMEM((RET_HEADS, RET_QK, RET_V), F32)]
    bmap = lambda j: jnp.where(j == 0, nc - 1, nc - 1 - j)
    o_b = pl.pallas_call(
        functools.partial(_ret_kernel, reverse=True, finalize=False),
        grid=(b, nc),
        in_specs=specs(bmap) + [lg_spec(1)],
        out_specs=pl.BlockSpec((None, c, vw), lambda bi, j: (bi, bmap(j), 0)),
        out_shape=jax.ShapeDtypeStruct((b, n, vw), F32),
        scratch_shapes=scratch,
        compiler_params=_cparams(("arbitrary", "arbitrary")),
        name="retention_bwd",
    )(p0, p0, p0, logit)
    fmap = lambda j: jnp.where(j == 0, nc - 1, j - 1)
    return pl.pallas_call(
        functools.partial(_ret_kernel, reverse=False, finalize=True),
        grid=(b, nc),
        in_specs=specs(fmap) + [
            lg_spec(0),
            pl.BlockSpec((None, c, vw), lambda bi, j: (bi, fmap(j), 0)),
            pl.BlockSpec((None, c, vw), lambda bi, j: (bi, fmap(j), E_RG * LANES // vw)),
            pl.BlockSpec((1, RET_V), lambda bi, j: (0, 0)),
        ],
        out_specs=pl.BlockSpec((None, c, vw), lambda bi, j: (bi, fmap(j), 0)),
        out_shape=jax.ShapeDtypeStruct((b, n, vw), BF16),
        scratch_shapes=scratch,
        compiler_params=_cparams(("arbitrary", "arbitrary")),
        name="retention_fwd",
    )(p0, p0, p0, logit, o_b, p0, rnorm_w.reshape(1, RET_V))


def _wattn_kernel(q_ref, kc_ref, vc_ref, kp_ref, vp_ref, k0_ref, v0_ref, kn_ref, vn_ref, sink_ref, o_ref,
                  *, n_tiles, group):
    i = pl.program_id(1)
    tq = q_ref.shape[0]
    nc = kc_ref.shape[0]
    kcat = jnp.concatenate([kc_ref[...], kp_ref[...], k0_ref[...], kn_ref[...]], axis=0)
    vcat = jnp.concatenate([vc_ref[...], vp_ref[...], v0_ref[...], vn_ref[...]], axis=0)
    nk = nc + 2 * WINDOW + tq
    kk = lax.broadcasted_iota(jnp.int32, (nk, tq), 0)
    qq = lax.broadcasted_iota(jnp.int32, (nk, tq), 1)
    koff = kk - (nc + WINDOW)
    far = 4 * tq
    miss = jnp.where(koff < 0, jnp.where(i > 0, 0, far), jnp.where(koff >= tq, jnp.where(i < n_tiles - 1, 0, far), 0))
    dist = jnp.where(kk < nc, 0, jnp.abs(qq - koff) + miss)
    bias1 = jnp.where(dist <= WINDOW, 0.0, NEG_INF).astype(F32)
    bias = jnp.concatenate([bias1, bias1], axis=1)
    lane = lax.broadcasted_iota(jnp.int32, (tq, LANES), 1)
    lo_half = lane < SW_DIM
    for hk in range(SW_KV_HEADS):
        kh = kcat[:, hk * SW_DIM:(hk + 1) * SW_DIM]
        kdup = jnp.concatenate([kh, kh], axis=1)
        for pg in range(group // 2):
            pair = hk * (group // 2) + pg
            qp = q_ref[:, pair * LANES:(pair + 1) * LANES]
            zero = jnp.zeros_like(qp)
            qstack = jnp.concatenate([jnp.where(lo_half, qp, zero), jnp.where(lo_half, zero, qp)], axis=0)
            s = _dot_nt(kdup, qstack) + bias
            sink = jnp.concatenate([jnp.broadcast_to(sink_ref[2 * pair][:, :1], (1, tq)),
                                    jnp.broadcast_to(sink_ref[2 * pair + 1][:, :1], (1, tq))], axis=1)
            sink = sink * LOG2E
            m = jnp.maximum(jnp.max(s, axis=0, keepdims=True), sink)
            p = jnp.exp2(s - m)
            den = jnp.sum(p, axis=0, keepdims=True) + jnp.exp2(sink - m)
            ot = _dot_tn(vcat, p.astype(BF16)) * (1.0 / den)
            oth = ot[hk * SW_DIM:(hk + 1) * SW_DIM]
            both = jnp.concatenate([oth[:, :tq], oth[:, tq:]], axis=0)
            o_ref[:, pair * LANES:(pair + 1) * LANES] = both.T.astype(o_ref.dtype)


def window_attention(p1, sinks, n_ctx):
    b, n, _ = p1.shape
    tq = ROW_TILE
    assert n_ctx == tq and WINDOW <= tq
    n_tiles = (n - n_ctx) // tq
    group = SW_Q_HEADS // SW_KV_HEADS
    qw = SW_Q_HEADS * SW_DIM
    kvw = SW_KV_HEADS * SW_DIM
    assert kvw == LANES and group % 2 == 0
    k_col = qw // kvw
    v_col = k_col + 1
    sink_b = jnp.broadcast_to(sinks.astype(F32).reshape(SW_Q_HEADS, 1, 1), (SW_Q_HEADS, 1, LANES))

    def kv(col, rows, tile):
        return pl.BlockSpec((None, rows, kvw), lambda bi, i: (bi, tile(i), col))

    per = tq // WINDOW
    assert tq % WINDOW == 0
    prev = lambda i: jnp.maximum(i * per - 1, 0)
    cur = lambda i: i
    nxt = lambda i: jnp.minimum((i + 1) * per, n_tiles * per - 1)
    ctx = lambda i: n_tiles
    return pl.pallas_call(
        functools.partial(_wattn_kernel, n_tiles=n_tiles, group=group),
        grid=(b, n_tiles),
        in_specs=[
            pl.BlockSpec((None, tq, qw), lambda bi, i: (bi, i, 0)),
            kv(k_col, tq, ctx), kv(v_col, tq, ctx), kv(k_col, WINDOW, prev), kv(v_col, WINDOW, prev),
            kv(k_col, tq, cur), kv(v_col, tq, cur), kv(k_col, WINDOW, nxt), kv(v_col, WINDOW, nxt),
            pl.BlockSpec((SW_Q_HEADS, 1, LANES), lambda bi, i: (0, 0, 0)),
        ],
        out_specs=pl.BlockSpec((None, tq, qw), lambda bi, i: (bi, i, 0)),
        out_shape=jax.ShapeDtypeStruct((b, n - n_ctx, qw), BF16),
        compiler_params=_cparams(("arbitrary", "arbitrary")),
        name="window_attn",
    )(p1, p1, p1, p1, p1, p1, p1, p1, p1, sink_b)


def _post_kernel(*refs, n_pieces):
    pieces = refs[:n_pieces]
    w_ref, h_ref, mod_ref, nw_ref, rw_ref, rb_ref, ho_ref, f_ref, lg_ref = refs[n_pieces:]
    m = None
    lo = 0
    for p in pieces:
        wdt = p.shape[1]
        t = _dot(p[...], w_ref[lo:lo + wdt, :])
        m = t if m is None else m + t
        lo += wdt
    md = mod_ref[...]
    hn = h_ref[...] + md[0:1] * m
    ho_ref[...] = hn
    f = _norm_mod(hn, nw_ref[...], md[1:2], md[2:3]).astype(BF16)
    f_ref[...] = f
    lg_ref[...] = _dot(f, rw_ref[...]) + rb_ref[...]


def post_mixer(pieces, w_out, h, mod, nw, router_w, router_b, n_lat_tiles):
    b, r, _ = pieces[0].shape
    d = h.shape[2]
    tm = ROW_TILE
    rw = jnp.zeros((d, LANES), BF16).at[:, :N_EXPERTS].set(router_w.astype(BF16))
    rb = jnp.full((1, LANES), NEG_INF, F32).at[0, :N_EXPERTS].set(router_b.astype(F32))
    row = lambda bi, i: (bi, i, 0)
    const2 = lambda bi, i: (0, 0)
    outs = pl.pallas_call(
        functools.partial(_post_kernel, n_pieces=len(pieces)),
        grid=(b, r // tm),
        in_specs=[pl.BlockSpec((None, tm, p.shape[2]), row) for p in pieces] + [
            pl.BlockSpec(w_out.shape, const2),
            pl.BlockSpec((None, tm, d), row),
            pl.BlockSpec((None, None, 3, d), lambda bi, i: (bi, jnp.where(i < n_lat_tiles, 1, 0), 0, 0)),
            pl.BlockSpec((1, d), const2),
            pl.BlockSpec((d, LANES), const2),
            pl.BlockSpec((1, LANES), const2),
        ],
        out_specs=[pl.BlockSpec((None, tm, d), row), pl.BlockSpec((None, tm, d), row),
                   pl.BlockSpec((None, tm, LANES), row)],
        out_shape=[jax.ShapeDtypeStruct((b, r, d), F32), jax.ShapeDtypeStruct((b, r, d), BF16),
                   jax.ShapeDtypeStruct((b, r, LANES), F32)],
        compiler_params=_cparams(("arbitrary", "arbitrary")),
        name="post_mixer",
    )(*pieces, w_out, h, mod, nw.reshape(1, d), rw, rb)
    return outs


def _router_kernel(lg_ref, info_ref, tile_ref, cnt_ref, base_sc):
    i = pl.program_id(0)
    tm = lg_ref.shape[0]

    @pl.when(i == 0)
    def _():
        base_sc[...] = jnp.zeros(base_sc.shape, F32)

    l = lg_ref[...]
    lane = lax.broadcasted_iota(jnp.int32, l.shape, 1)
    lanef = lane.astype(F32)
    hots, vals, idxs = [], [], []
    for _ in range(TOP_K):
        m = jnp.max(l, axis=1, keepdims=True)
        idx = jnp.min(jnp.where(l == m, lanef, float(LANES)), axis=1, keepdims=True)
        hot = lanef == idx
        hots.append(hot)
        vals.append(m)
        idxs.append(idx)
        l = jnp.where(hot, -jnp.inf, l)
    es = [jnp.exp(v - vals[0]) for v in vals]
    tot = es[0] + es[1] + es[2] + es[3]
    sel = jnp.zeros(l.shape, F32)
    for hot in hots:
        sel = sel + jnp.where(hot, 1.0, 0.0)
    ri = lax.broadcasted_iota(jnp.int32, (tm, tm), 0)
    ci = lax.broadcasted_iota(jnp.int32, (tm, tm), 1)
    earlier = jnp.where(ri > ci, 1.0, 0.0).astype(BF16)
    before = _dot(earlier, sel.astype(BF16))
    cnt = jnp.sum(sel, axis=0, keepdims=True)
    run = jnp.floor((cnt + (RUN_ROWS - 1.0)) * (1.0 / RUN_ROWS)) * RUN_ROWS
    li = lax.broadcasted_iota(jnp.int32, (LANES, LANES), 0)
    lj = lax.broadcasted_iota(jnp.int32, (LANES, LANES), 1)
    lower = jnp.where(li < lj, 1.0, 0.0).astype(BF16)
    run_off = _dot(jnp.broadcast_to(run, (8, LANES)).astype(BF16), lower)[0:1]
    where_ = before + run_off
    info = jnp.zeros(l.shape, F32)
    for k in range(TOP_K):
        row = jnp.sum(jnp.where(hots[k], where_, 0.0), axis=1, keepdims=True)
        info = jnp.where(lane == k, idxs[k], info)
        info = jnp.where(lane == TOP_K + k, row, info)
        info = jnp.where(lane == 2 * TOP_K + k, es[k] / tot, info)
    info_ref[...] = info
    sub = lax.broadcasted_iota(jnp.int32, tile_ref.shape, 0)
    tile_ref[...] = jnp.where(sub == 0, cnt, jnp.where(sub == 1, base_sc[...], jnp.where(sub == 2, run_off, 0.0)))
    base_sc[...] = base_sc[...] + jnp.floor((cnt + (SLOT_ALIGN - 1.0)) * (1.0 / SLOT_ALIGN)) * SLOT_ALIGN
    cnt_ref[...] = jnp.broadcast_to(base_sc[...], cnt_ref.shape)


def route(logits):
    t = logits.shape[0]
    tm = ROUTE_TILE
    nt = t // tm
    info, tile, cnt = pl.pallas_call(
        _router_kernel,
        grid=(nt,),
        in_specs=[pl.BlockSpec((tm, LANES), lambda i: (i, 0))],
        out_specs=[pl.BlockSpec((tm, LANES), lambda i: (i, 0)), pl.BlockSpec((None, 8, LANES), lambda i: (i, 0, 0)),
                   pl.BlockSpec((8, LANES), lambda i: (0, 0))],
        out_shape=[jax.ShapeDtypeStruct((t, LANES), F32), jax.ShapeDtypeStruct((nt, 8, LANES), F32),
                   jax.ShapeDtypeStruct((8, LANES), F32)],
        scratch_shapes=[pltpu.VMEM((1, LANES), F32)],
        compiler_params=_cparams(("arbitrary",)),
        name="router",
    )(logits)
    counts = cnt[0, :N_EXPERTS].astype(jnp.int32)
    padded = ((counts + RUN_ROWS + MOE_TILE - 1) // MOE_TILE) * MOE_TILE
    pad_ends = jnp.cumsum(padded)
    pad_starts = pad_ends - padded
    worst = t * TOP_K + nt * N_EXPERTS * (SLOT_ALIGN - 1) + N_EXPERTS * (RUN_ROWS + MOE_TILE - 1)
    n_blocks = -(-worst // MOE_TILE)
    block_lo = jnp.arange(n_blocks, dtype=jnp.int32) * MOE_TILE
    block_expert = jnp.minimum(jnp.sum((pad_ends[None, :] <= block_lo[:, None]).astype(jnp.int32), axis=1),
                               N_EXPERTS - 1)
    n_used = (pad_ends[-1] // MOE_TILE).astype(jnp.int32).reshape(1)
    tcnt = tile[:, 0, :N_EXPERTS].astype(jnp.int32)
    slot0 = pad_starts[None, :] + tile[:, 1, :N_EXPERTS].astype(jnp.int32)
    runs = (tcnt + RUN_ROWS - 1) // RUN_ROWS
    roff = tile[:, 2, :N_EXPERTS].astype(jnp.int32)
    table = jnp.concatenate([slot0, runs, roff, jnp.zeros_like(roff)], axis=1).reshape(nt, 1, 4 * N_EXPERTS)
    return info, table, block_expert, n_used, n_blocks


def _run_copies(table_ref, make):
    def for_each_run(fn):
        def per_expert(e, carry):
            slot0 = table_ref[0, e]
            row0 = table_ref[0, 2 * N_EXPERTS + e]

            def one(j, c):
                fn(make(pl.multiple_of(slot0 + j * RUN_ROWS, SLOT_ALIGN),
                        pl.multiple_of(row0 + j * RUN_ROWS, RUN_ROWS)))
                return c

            return lax.fori_loop(0, table_ref[0, N_EXPERTS + e], one, carry)

        lax.fori_loop(0, N_EXPERTS, per_expert, 0)

    for_each_run(lambda cp: cp.start())
    for_each_run(lambda cp: cp.wait())


def _staging_onehots(info):
    tm = info.shape[0]
    col = lax.broadcasted_iota(jnp.int32, (tm, STAGE_ROWS), 1).astype(F32)
    return [col == info[:, TOP_K + k:TOP_K + k + 1] for k in range(TOP_K)]


def _dispatch_kernel(table_ref, info_ref, f_ref, xin_ref, xbuf_ref, stage, sem):
    del xin_ref
    info = info_ref[...]
    hots = _staging_onehots(info)
    wgt = jnp.zeros(hots[0].shape, F32)
    for k in range(TOP_K):
        wgt = jnp.where(hots[k], info[:, 2 * TOP_K + k:2 * TOP_K + k + 1], wgt)
    sel = jnp.logical_or(jnp.logical_or(hots[0], hots[1]), jnp.logical_or(hots[2], hots[3]))
    place = jnp.where(sel, 1.0, 0.0).astype(BF16)
    half = f_ref.shape[1] // 2
    for c in range(half // STAGE_COLS):
        lo = c * STAGE_COLS
        stage[:, lo:lo + STAGE_COLS] = _pack_halves(_dot_tn(place, f_ref[:, lo:lo + STAGE_COLS]),
                                                    _dot_tn(place, f_ref[:, half + lo:half + lo + STAGE_COLS]))
    gates = jnp.broadcast_to(jnp.sum(wgt, axis=0, keepdims=True), (LANES, wgt.shape[1])).T
    stage[:, half:half + LANES] = lax.bitcast_convert_type(gates, jnp.uint32)

    def make(slot, row):
        return pltpu.make_async_copy(stage.at[pl.ds(row, RUN_ROWS), :], xbuf_ref.at[pl.ds(slot, RUN_ROWS), :], sem)

    _run_copies(table_ref, make)


def dispatch(f, info, table, n_slots):
    t, d = f.shape
    tm = ROUTE_TILE
    xw = d // 2 + LANES
    x0 = jnp.zeros((n_slots, xw), jnp.uint32)
    return pl.pallas_call(
        _dispatch_kernel,
        grid=(t // tm,),
        in_specs=[
            pl.BlockSpec((None, 1, 4 * N_EXPERTS), lambda i: (i, 0, 0), memory_space=pltpu.SMEM),
            pl.BlockSpec((tm, LANES), lambda i: (i, 0)),
            pl.BlockSpec((tm, d), lambda i: (i, 0)),
            pl.BlockSpec(memory_space=pl.ANY),
        ],
        out_specs=pl.BlockSpec(memory_space=pl.ANY),
        out_shape=jax.ShapeDtypeStruct((n_slots, xw), jnp.uint32),
        scratch_shapes=[pltpu.VMEM((STAGE_ROWS, xw), jnp.uint32), pltpu.SemaphoreType.DMA(())],
        input_output_aliases={3: 0},
        compiler_params=_cparams(("arbitrary",)),
        name="dispatch",
    )(table, info, f, x0)


def _ffn_kernel(be_ref, nb_ref, x_ref, w1_ref, b1_ref, w2_ref, b2_ref, o_ref, w1_sc, w2_sc):
    i = pl.program_id(0)
    used = i < nb_ref[0]
    fresh = jnp.logical_or(i == 0, be_ref[i] != be_ref[jnp.maximum(i - 1, 0)])

    @pl.when(jnp.logical_and(used, fresh))
    def _():
        w1_sc[...] = w1_ref[...].astype(BF16)
        w2_sc[...] = w2_ref[...].astype(BF16)

    @pl.when(used)
    def _():
        pw = o_ref.shape[1]
        x = _unpack_pairs(x_ref[:, :pw]).astype(BF16)
        gate = lax.bitcast_convert_type(x_ref[:, pw:], F32)[:, :1]
        dff = w2_sc.shape[0]
        acc = None
        for c in range(dff // FF_CHUNK):
            lo = c * FF_CHUNK
            hg = _dot(x, w1_sc[:, lo:lo + FF_CHUNK]) + b1_ref[:, lo:lo + FF_CHUNK]
            hl = _dot(x, w1_sc[:, dff + lo:dff + lo + FF_CHUNK]) + b1_ref[:, dff + lo:dff + lo + FF_CHUNK]
            glu = jnp.minimum(hg, SWIGLU_LIMIT)
            lin = jnp.clip(hl, -SWIGLU_LIMIT, SWIGLU_LIMIT)
            act = glu * _sigmoid(SWIGLU_ALPHA * glu) * (lin + 1.0)
            t = _dot(act.astype(BF16), w2_sc[lo:lo + FF_CHUNK, :])
            acc = t if acc is None else acc + t
        o_ref[...] = _pack_pairs((acc + b2_ref[...]) * gate)

    @pl.when(jnp.logical_not(used))
    def _():
        o_ref[...] = jnp.zeros(o_ref.shape, o_ref.dtype)


def expert_ffn(x_buf, block_expert, n_used, layer, w1, b1, w2, b2):
    p, xw = x_buf.shape
    w = xw - LANES
    nl, ne, d, f2 = w1.shape
    dff = w2.shape[2]
    tm = MOE_TILE
    grid_spec = pltpu.PrefetchScalarGridSpec(
        num_scalar_prefetch=2,
        grid=(p // tm,),
        in_specs=[
            pl.BlockSpec((tm, xw), lambda i, be, nb: (i, 0)),
            pl.BlockSpec((None, None, d, f2), lambda i, be, nb: (layer, be[i], 0, 0)),
            pl.BlockSpec((None, None, 1, f2), lambda i, be, nb: (layer, be[i], 0, 0)),
            pl.BlockSpec((None, None, dff, d), lambda i, be, nb: (layer, be[i], 0, 0)),
            pl.BlockSpec((None, None, 1, d), lambda i, be, nb: (layer, be[i], 0, 0)),
        ],
        out_specs=pl.BlockSpec((tm, w), lambda i, be, nb: (i, 0)),
        scratch_shapes=[pltpu.VMEM((d, f2), BF16), pltpu.VMEM((dff, d), BF16)],
    )
    return pl.pallas_call(
        _ffn_kernel,
        grid_spec=grid_spec,
        out_shape=jax.ShapeDtypeStruct((p, w), jnp.uint32),
        compiler_params=pltpu.CompilerParams(dimension_semantics=("arbitrary",), vmem_limit_bytes=FFN_VMEM_LIMIT),
        name="expert_ffn",
    )(block_expert, n_used, x_buf, w1, b1.reshape(nl, ne, 1, f2), w2, b2.reshape(nl, ne, 1, d))


def _combine_kernel(*refs, final):
    if final:
        table_ref, h_ref, info_ref, g2a_ref, g2b_ref, nw_ref, ybuf_ref, o_ref, stage, sem = refs
    else:
        table_ref, h_ref, info_ref, g2a_ref, g2b_ref, ybuf_ref, o_ref, stage, sem = refs

    @pl.when(pl.program_id(0) == 0)
    def _():
        stage[...] = jnp.zeros(stage.shape, stage.dtype)

    def make(slot, row):
        return pltpu.make_async_copy(ybuf_ref.at[pl.ds(slot, RUN_ROWS), :], stage.at[pl.ds(row, RUN_ROWS), :], sem)

    _run_copies(table_ref, make)

    hots = _staging_onehots(info_ref[...])
    sel = jnp.logical_or(jnp.logical_or(hots[0], hots[1]), jnp.logical_or(hots[2], hots[3]))
    place = jnp.where(sel, 1.0, 0.0).astype(BF16)
    half = h_ref.shape[1] // 2
    lows, highs = [], []
    for c in range(half // STAGE_COLS):
        y_lo, y_hi = _unpack_halves(stage[:, c * STAGE_COLS:(c + 1) * STAGE_COLS])
        lows.append(_dot(place, y_lo.astype(BF16)))
        highs.append(_dot(place, y_hi.astype(BF16)))
    acc = jnp.concatenate(lows + highs, axis=1)
    sub = h_ref.shape[0] // 2
    g2 = jnp.concatenate([jnp.broadcast_to(g2a_ref[...], (sub, acc.shape[1])),
                          jnp.broadcast_to(g2b_ref[...], (sub, acc.shape[1]))], axis=0)
    hn = h_ref[...] + g2 * acc
    if final:
        y = hn * lax.rsqrt(jnp.mean(hn * hn, axis=-1, keepdims=True) + EPS)
        hn = y * nw_ref[...]
    o_ref[...] = hn


def combine(h, y_buf, info, table, g2, n_lat_tiles, final_w=None):
    b, r, d = h.shape
    tm = ROUTE_TILE
    rows_per_batch = r // ROW_TILE

    def g2_spec(j):
        def index(i):
            s = 2 * i + j
            return (s // rows_per_batch, jnp.where(s % rows_per_batch < n_lat_tiles, 1, 0), 0, 0)
        return pl.BlockSpec((None, None, 1, d), index)

    final = final_w is not None
    in_specs = [
        pl.BlockSpec((None, 1, 4 * N_EXPERTS), lambda i: (i, 0, 0), memory_space=pltpu.SMEM),
        pl.BlockSpec((tm, d), lambda i: (i, 0)),
        pl.BlockSpec((tm, LANES), lambda i: (i, 0)),
        g2_spec(0), g2_spec(1),
    ]
    args = [table, h.reshape(b * r, d), info, g2, g2]
    if final:
        in_specs.append(pl.BlockSpec((1, d), lambda i: (0, 0)))
        args.append(final_w.reshape(1, d))
    in_specs.append(pl.BlockSpec(memory_space=pl.ANY))
    args.append(y_buf)
    out = pl.pallas_call(
        functools.partial(_combine_kernel, final=final),
        grid=(b * r // tm,),
        in_specs=in_specs,
        out_specs=pl.BlockSpec((tm, d), lambda i: (i, 0)),
        out_shape=jax.ShapeDtypeStruct((b * r, d), F32),
        scratch_shapes=[pltpu.VMEM((STAGE_ROWS, d // 2), jnp.uint32), pltpu.SemaphoreType.DMA(())],
        compiler_params=_cparams(("arbitrary",)),
        name="combine",
    )(*args)
    return out.reshape(b, r, d)


def moe_layer(h, f, logits, g2, n_lat_tiles, layer, w1, b1, w2, b2, final_w=None):
    b, r, d = h.shape
    t = b * r
    info, table, block_expert, n_used, n_blocks = route(logits.reshape(t, LANES))
    x_buf = dispatch(f.reshape(t, d), info, table, n_blocks * MOE_TILE)
    y_buf = expert_ffn(x_buf, block_expert, n_used, layer, w1, b1, w2, b2)
    return combine(h, y_buf, info, table, g2, n_lat_tiles, final_w)


def _mod_table(rows, n_batch):
    lat = jnp.stack([r[:n_batch] for r in rows], axis=1)
    ctx = jnp.stack([r[n_batch] for r in rows], axis=0)
    return jnp.stack([jnp.broadcast_to(ctx[None], lat.shape), lat], axis=1)


def kernel(x, c, ctx, c_ctx, mod_w, mod_b, norm1_w, norm2_w, even_w_in, even_w_out, diff_lam, diff_norm_w,
           ret_decay_logit, ret_norm_w, odd_w_qkv, odd_w_out, odd_sinks, router_w, router_b,
           moe_w1, moe_b1, moe_w2, moe_b2, final_norm_w):
    b, s, d = x.shape
    n_ctx = ctx.shape[1]
    n_lat_tiles = s // ROW_TILE
    assert n_ctx == ROW_TILE and s % ROW_TILE == 0 and b + 1 <= 8

    cvec = jnp.zeros((8, d), F32).at[:b].set(c).at[b].set(c_ctx)
    mods = adaln_all(cvec, mod_w, mod_b)
    rope_tabs = rope_tables(s, n_ctx)

    sh1, sc1, g1, sh2, sc2, g2 = jnp.split(mods[0], 6, axis=-1)
    h = jnp.concatenate([x, ctx], axis=1)
    groups0 = ((True, 0.125 * LOG2E),) * 4 + ((True, 1.0),) * 4 + ((False, 1.0),) * 4 \
        + ((False, 0.125),) * 2 + ((False, 1.0),) * 10
    p0 = proj_in(h, _mod_table([sh1, sc1], b), norm1_w[0], even_w_in[0].astype(BF16), rope_tabs, groups0,
                 n_lat_tiles)
    lam_init = 0.8 - 0.6 * math.exp(-0.3 * 0)
    d_out = diff_attention(p0, diff_lam[0], diff_norm_w[0], lam_init, n_ctx)
    r_out = retention(p0, ret_decay_logit[0], ret_norm_w[0])
    h, f, logits = post_mixer([d_out, r_out], even_w_out[0].astype(BF16), h, _mod_table([g1, sh2, sc2], b),
                              norm2_w[0], router_w[0], router_b[0], n_lat_tiles)
    h = moe_layer(h, f, logits, _mod_table([g2], b), n_lat_tiles, 0, moe_w1, moe_b1, moe_w2, moe_b2)

    sh1, sc1, g1, sh2, sc2, g2 = jnp.split(mods[1], 6, axis=-1)
    groups1 = ((True, 0.125 * LOG2E),) * 8 + ((True, 1.0),) + ((False, 1.0),)
    p1 = proj_in(h, _mod_table([sh1, sc1], b), norm1_w[1], odd_w_qkv[0].astype(BF16), rope_tabs, groups1,
                 n_lat_tiles)
    a_out = window_attention(p1, odd_sinks[0], n_ctx)
    h_lat, f, logits = post_mixer([a_out], odd_w_out[0].astype(BF16), h, _mod_table([g1, sh2, sc2], b),
                                  norm2_w[1], router_w[1], router_b[1], n_lat_tiles)
    return moe_layer(h_lat, f, logits, _mod_table([g2], b), n_lat_tiles, 1, moe_w1, moe_b1, moe_w2, moe_b2,
                     final_w=final_norm_w)
```
